```python
import jax, jax.numpy as jnp
from jax import lax
import numpy as np

D_MODEL = 1024
BATCH = 32
SEQ = 2048
DEPTH = 1
DEC_BATCH = 16
DEC_SEQ = 16
PAST_LEN = 1024

CHUNK = 64
D_RNN = D_MODEL
N_LRU_BLOCKS = 16
LRU_BLOCK = D_RNN // N_LRU_BLOCKS
CONV_WIDTH = 4
LRU_C = 8.0
MLP_CHUNK = 128
N_MLP_GROUPS = 4
MLP_GROUP_DIM = D_MODEL // 8
D_MLP = N_MLP_GROUPS * MLP_GROUP_DIM
SPLITS = (D_RNN, 2 * D_RNN, 2 * D_RNN + D_MLP, 2 * D_RNN + 2 * D_MLP, 2 * D_RNN + 2 * D_MLP + D_MODEL)
N_IN = 2 * D_RNN + 2 * D_MLP + 2 * D_MODEL
N_EXPERTS = 32
TOP_K = 4
D_EXPERT = D_MODEL
SWIGLU_LIMIT = 7.0
SWIGLU_ALPHA = 1.702
MOE_BLOCK = 128
RMS_EPS = 1e-6
LN_EPS = 1e-5

kernel_name = "hawk_gmlp_moe_streaming_step"


def rms_norm(x, g):
    xf = x.astype(jnp.float32)
    y = xf * lax.rsqrt(jnp.mean(xf * xf, axis=-1, keepdims=True) + RMS_EPS)
    return (y * g.astype(jnp.float32)).astype(x.dtype)


def layer_norm(x, g, b):
    xf = x.astype(jnp.float32)
    mu = jnp.mean(xf, axis=-1, keepdims=True)
    xc = xf - mu
    y = xc * lax.rsqrt(jnp.mean(xc * xc, axis=-1, keepdims=True) + LN_EPS)
    return (y * g.astype(jnp.float32) + b.astype(jnp.float32)).astype(x.dtype)


def causal_conv(xb, buf, conv_w, conv_b):
    S = xb.shape[1]
    xp = jnp.concatenate([buf.astype(xb.dtype), xb], axis=1)
    y = conv_b
    for k in range(CONV_WIDTH):
        y = y + xp[:, k:k + S] * conv_w[k]
    return y, xp[:, -(CONV_WIDTH - 1):]


def rg_lru(xc, h0, start, w_rg, b_rg, w_ig, b_ig, lru_lambda):
    B, S, _ = xc.shape
    xblk = xc.reshape(B, S, N_LRU_BLOCKS, LRU_BLOCK)
    r = jax.nn.sigmoid((jnp.einsum('bshi,hij->bshj', xblk, w_rg).reshape(B, S, D_RNN) + b_rg).astype(jnp.float32))
    i = jax.nn.sigmoid((jnp.einsum('bshi,hij->bshj', xblk, w_ig).reshape(B, S, D_RNN) + b_ig).astype(jnp.float32))
    log_a = -LRU_C * r * jax.nn.softplus(-lru_lambda.astype(jnp.float32))
    a = jnp.exp(log_a)
    mult = jnp.sqrt(jnp.maximum(1.0 - jnp.exp(2.0 * log_a), 0.0))
    pos = start + jnp.arange(S)
    mult = jnp.where((pos == 0)[None, :, None], 1.0, mult)
    bterm = mult * i * xc.astype(jnp.float32)
    bterm = bterm.at[:, 0].add(a[:, 0] * h0.astype(jnp.float32))

    def combine(left, right):
        a1, b1 = left
        a2, b2 = right
        return a1 * a2, a2 * b1 + b2

    _, h = lax.associative_scan(combine, (a, bterm), axis=1)
    return h, h[:, -1]


def chunk_spatial_mix(v, w_s, b_s):
    B, S = v.shape[0], v.shape[1]
    n_chunks = -(-S // MLP_CHUNK)
    pad = n_chunks * MLP_CHUNK - S
    vp = jnp.pad(v, ((0, 0), (0, pad), (0, 0), (0, 0)))
    vp = vp.reshape(B, n_chunks, MLP_CHUNK, N_MLP_GROUPS, MLP_GROUP_DIM)
    mask = jnp.tril(jnp.ones((MLP_CHUNK, MLP_CHUNK), dtype=bool))
    ws = jnp.where(mask[None], w_s, 0)
    out = jnp.einsum('gts,bcsgd->bctgd', ws, vp) + b_s.T[None, None, :, :, None]
    return out.reshape(B, n_chunks * MLP_CHUNK, N_MLP_GROUPS, MLP_GROUP_DIM)[:, :S]


def mixer(xn, conv_buf, h0, start, w_in, conv_w, conv_b, w_rg, b_rg, w_ig, b_ig, lru_lambda,
          ln_v_g, ln_v_b, w_s, b_s, w_proj_a, w_proj_b, w_out):
    B, S, _ = xn.shape
    z = xn @ w_in
    xa, ga, u, v, gate_a, gate_b = jnp.split(z, SPLITS, axis=-1)
    xc, new_buf = causal_conv(xa, conv_buf, conv_w, conv_b)
    h, h_last = rg_lru(xc, h0, start, w_rg, b_rg, w_ig, b_ig, lru_lambda)
    ya = (h * jax.nn.gelu(ga.astype(jnp.float32))).astype(xn.dtype) @ w_proj_a
    u = jax.nn.gelu(u)
    v = layer_norm(jax.nn.gelu(v), ln_v_g, ln_v_b)
    vmix = chunk_spatial_mix(v.reshape(B, S, N_MLP_GROUPS, MLP_GROUP_DIM), w_s, b_s).reshape(B, S, D_MLP)
    yb = (u * vmix) @ w_proj_b
    merged = jax.nn.sigmoid(gate_a) * ya + jax.nn.sigmoid(gate_b) * yb
    return merged @ w_out, new_buf, h_last, v


def moe(x2d, w_router, b_router, w_gate_up, b_gate_up, w_down, b_down):
    T, D = x2d.shape
    n_assign = T * TOP_K
    n_blocks = -(-(n_assign + N_EXPERTS * (MOE_BLOCK - 1)) // MOE_BLOCK)
    n_rows = n_blocks * MOE_BLOCK
    logits = (x2d @ w_router + b_router).astype(jnp.float32)
    top_logits, top_idx = lax.top_k(logits, TOP_K)
    gates = jax.nn.softmax(top_logits, axis=-1)
    flat_e = top_idx.reshape(-1).astype(jnp.int32)
    order = jnp.argsort(flat_e)
    sorted_e = flat_e[order]
    counts = jnp.bincount(flat_e, length=N_EXPERTS)
    padded = (counts + MOE_BLOCK - 1) // MOE_BLOCK * MOE_BLOCK
    grp_start = jnp.cumsum(counts) - counts
    pad_end = jnp.cumsum(padded)
    pad_start = pad_end - padded
    dest = pad_start[sorted_e] + jnp.arange(n_assign) - grp_start[sorted_e]
    row_tok = jnp.full((n_rows,), T, jnp.int32).at[dest].set((order // TOP_K).astype(jnp.int32))
    row_gate = jnp.zeros((n_rows,), jnp.float32).at[dest].set(gates.reshape(-1)[order])
    blk_exp = jnp.minimum(jnp.searchsorted(pad_end, jnp.arange(n_blocks) * MOE_BLOCK, side='right'), N_EXPERTS - 1)
    x_pad = jnp.concatenate([x2d, jnp.zeros((1, D), x2d.dtype)], axis=0)

    def expert_block(args):
        tok, e = args
        hb = x_pad[tok]
        gu = (hb @ w_gate_up[e] + b_gate_up[e]).astype(jnp.float32)
        gate, up = gu[:, :D_EXPERT], gu[:, D_EXPERT:]
        gate = jnp.minimum(gate, SWIGLU_LIMIT)
        up = jnp.clip(up, -SWIGLU_LIMIT, SWIGLU_LIMIT)
        act = (up + 1.0) * (gate * jax.nn.sigmoid(SWIGLU_ALPHA * gate))
        return (act.astype(x2d.dtype) @ w_down[e] + b_down[e]).astype(jnp.float32)

    outs = lax.map(expert_block, (row_tok.reshape(n_blocks, MOE_BLOCK), blk_exp))
    y = jnp.zeros((T + 1, D), jnp.float32).at[row_tok].add(outs.reshape(n_rows, D) * row_gate[:, None])
    return y[:T].astype(x2d.dtype)


def layer(x, conv_buf, h0, start, norm1_g, w_in, conv_w, conv_b, w_rg, b_rg, w_ig, b_ig, lru_lambda,
          ln_v_g, ln_v_b, w_s, b_s, w_proj_a, w_proj_b, w_out, norm2_g, w_router, b_router,
          w_gate_up, b_gate_up, w_down, b_down):
    B, S, D = x.shape
    mix, new_buf, h_last, v_rows = mixer(rms_norm(x, norm1_g), conv_buf, h0, start, w_in, conv_w, conv_b,
                                         w_rg, b_rg, w_ig, b_ig, lru_lambda, ln_v_g, ln_v_b, w_s, b_s,
                                         w_proj_a, w_proj_b, w_out)
    x = x + mix
    x = x + moe(rms_norm(x, norm2_g).reshape(B * S, D), w_router, b_router,
                w_gate_up, b_gate_up, w_down, b_down).reshape(B, S, D)
    return x, new_buf, h_last, v_rows


def setup_inputs(seed: int = 0) -> dict:
    key = jax.random.key(seed)
    ks = iter(jax.random.split(key, 40))

    def nrm(shape, scale):
        return jax.random.normal(next(ks), shape, jnp.float32) * scale

    u_a = jax.random.uniform(next(ks), (DEPTH, D_RNN), jnp.float32, minval=0.9, maxval=0.999)
    return {
        "x_prompt": nrm((BATCH, SEQ, D_MODEL), 1.0),
        "x_sample": nrm((DEC_BATCH, DEC_SEQ, D_MODEL), 1.0),
        "state_conv": nrm((DEPTH, DEC_BATCH, CONV_WIDTH - 1, D_RNN), 1.0),
        "state_h": nrm((DEPTH, DEC_BATCH, D_RNN), 0.5),
        "norm1_g": 1.0 + nrm((DEPTH, D_MODEL), 0.02),
        "w_in": nrm((DEPTH, D_MODEL, N_IN), D_MODEL ** -0.5),
        "conv_w": nrm((DEPTH, CONV_WIDTH, D_RNN), CONV_WIDTH ** -0.5),
        "conv_b": nrm((DEPTH, D_RNN), 0.01),
        "w_rg": nrm((DEPTH, N_LRU_BLOCKS, LRU_BLOCK, LRU_BLOCK), LRU_BLOCK ** -0.5),
        "b_rg": nrm((DEPTH, D_RNN), 0.01),
        "w_ig": nrm((DEPTH, N_LRU_BLOCKS, LRU_BLOCK, LRU_BLOCK), LRU_BLOCK ** -0.5),
        "b_ig": nrm((DEPTH, D_RNN), 0.01),
        "lru_lambda": jnp.log(u_a) - jnp.log1p(-u_a),
        "ln_v_g": 1.0 + nrm((DEPTH, D_MLP), 0.02),
        "ln_v_b": nrm((DEPTH, D_MLP), 0.01),
        "w_s": nrm((DEPTH, N_MLP_GROUPS, MLP_CHUNK, MLP_CHUNK), MLP_CHUNK ** -0.5),
        "b_s": 1.0 + nrm((DEPTH, N_MLP_GROUPS, MLP_CHUNK), 0.1),
        "w_proj_a": nrm((DEPTH, D_RNN, D_MODEL), D_RNN ** -0.5),
        "w_proj_b": nrm((DEPTH, D_MLP, D_MODEL), D_MLP ** -0.5),
        "w_out": nrm((DEPTH, D_MODEL, D_MODEL), D_MODEL ** -0.5),
        "norm2_g": 1.0 + nrm((DEPTH, D_MODEL), 0.02),
        "w_router": nrm((DEPTH, D_MODEL, N_EXPERTS), D_MODEL ** -0.5),
        "b_router": nrm((DEPTH, N_EXPERTS), 0.01),
        "w_gate_up": nrm((DEPTH, N_EXPERTS, D_MODEL, 2 * D_EXPERT), D_MODEL ** -0.5),
        "b_gate_up": nrm((DEPTH, N_EXPERTS, 2 * D_EXPERT), 0.01),
        "w_down": nrm((DEPTH, N_EXPERTS, D_EXPERT, D_MODEL), D_EXPERT ** -0.5),
        "b_down": nrm((DEPTH, N_EXPERTS, D_MODEL), 0.01),
        "final_norm_g": 1.0 + nrm((D_MODEL,), 0.02),
    }


def reference(x_prompt, x_sample, state_conv, state_h, norm1_g, w_in, conv_w, conv_b, w_rg, b_rg,
              w_ig, b_ig, lru_lambda, ln_v_g, ln_v_b, w_s, b_s, w_proj_a, w_proj_b, w_out, norm2_g,
              w_router, b_router, w_gate_up, b_gate_up, w_down, b_down, final_norm_g):
    yp, ys = x_prompt, x_sample
    conv_p, h_p, conv_s, h_s, v_s = [], [], [], [], []
    for l in range(DEPTH):
        lp = (norm1_g[l], w_in[l], conv_w[l], conv_b[l], w_rg[l], b_rg[l], w_ig[l], b_ig[l], lru_lambda[l],
              ln_v_g[l], ln_v_b[l], w_s[l], b_s[l], w_proj_a[l], w_proj_b[l], w_out[l], norm2_g[l],
              w_router[l], b_router[l], w_gate_up[l], b_gate_up[l], w_down[l], b_down[l])
        nb = yp.shape[0]
        yp, buf_p, hl_p, _ = layer(yp, jnp.zeros((nb, CONV_WIDTH - 1, D_RNN), yp.dtype),
                                   jnp.zeros((nb, D_RNN), jnp.float32), 0, *lp)
        conv_p.append(buf_p)
        h_p.append(hl_p)
        ys, buf_s, hl_s, v_rows = layer(ys, state_conv[l], state_h[l], PAST_LEN, *lp)
        conv_s.append(buf_s)
        h_s.append(hl_s)
        v_s.append(v_rows)
    y_prompt = rms_norm(yp, final_norm_g)
    y_sample = rms_norm(ys, final_norm_g)
    return (y_prompt, y_sample, jnp.stack(conv_p), jnp.stack(h_p), jnp.stack(conv_s), jnp.stack(h_s), jnp.stack(v_s))
```

```python
import functools

import jax
import jax.numpy as jnp
from jax import lax
from jax.experimental import pallas as pl
from jax.experimental.pallas import tpu as pltpu

F32 = jnp.float32
BF16 = jnp.bfloat16
I32 = jnp.int32

D_MODEL = 1024
D_RNN = D_MODEL
N_LRU_BLOCKS = 16
LRU_BLOCK = D_RNN // N_LRU_BLOCKS
CONV_WIDTH = 4
LRU_C = 8.0
MLP_CHUNK = 128
N_MLP_GROUPS = 4
MLP_GROUP_DIM = D_MODEL // 8
D_MLP = N_MLP_GROUPS * MLP_GROUP_DIM
N_IN = 2 * D_RNN + 2 * D_MLP + 2 * D_MODEL
N_EXPERTS = 32
TOP_K = 4
D_EXPERT = D_MODEL
SWIGLU_LIMIT = 7.0
SWIGLU_ALPHA = 1.702
RMS_EPS = 1e-6
LN_EPS = 1e-5
PAST_LEN = 1024

SUBLANES = 8
LANES = 128
MXU_DIM = 256
VMEM_LIMIT_BYTES = 56 * 1024 * 1024

MIX_ROWS = 256
ROUTE_CHUNK = 256
ROW_BLOCK = 256
TOK_TILE = 256
GATE_PACK = MXU_DIM // LRU_BLOCK


def _const_spec(shape):
    nd = len(shape)
    return pl.BlockSpec(shape, lambda *_: (0,) * nd, pipeline_mode=pl.Buffered(1))


def _mixer_body(x_ref, conv0_ref, h0_ref, g1_ref, win_ref, convw_ref, convb_ref, wg_ref, brg_ref, big_ref,
                lam_ref, lng_ref, lnb_ref, ws_ref, bst_ref, wpa_ref, wpb_ref, wout_ref, g2_ref, wrt_ref,
                brt_ref, *rest, nseq, seg, mc, start_zero, want_v):
    if want_v:
        x1_ref, xn2_ref, lgt_ref, convl_ref, hl_ref, v_ref, xa_buf, h_carry = rest
    else:
        x1_ref, xn2_ref, lgt_ref, convl_ref, hl_ref, xa_buf, h_carry = rest
        v_ref = None
    s = pl.program_id(1)
    rows = nseq * seg

    @pl.when(s == 0)
    def _():
        xa_buf[:, 0:SUBLANES, :] = conv0_ref[...]
        h_carry[...] = h0_ref[...].reshape(nseq, D_RNN)

    x = x_ref[...].reshape(rows, D_MODEL)
    ms = jnp.mean(x * x, axis=-1, keepdims=True)
    xn = (x * lax.rsqrt(ms + RMS_EPS) * g1_ref[...]).astype(BF16)
    z = jnp.dot(xn, win_ref[...], preferred_element_type=F32)

    xa = z[:, 0:D_RNN]
    ga = z[:, D_RNN:2 * D_RNN]
    zu = z[:, 2 * D_RNN:2 * D_RNN + D_MLP]
    zv = z[:, 2 * D_RNN + D_MLP:2 * D_RNN + 2 * D_MLP]
    gate_a = z[:, 2 * D_RNN + 2 * D_MLP:2 * D_RNN + 2 * D_MLP + D_MODEL]
    gate_b = z[:, 2 * D_RNN + 2 * D_MLP + D_MODEL:N_IN]

    cw = convw_ref[...]
    xc_parts = []
    for q in range(nseq):
        xa_buf[q, SUBLANES:SUBLANES + seg, :] = xa[q * seg:(q + 1) * seg]
        acc = convb_ref[...] + xa[q * seg:(q + 1) * seg] * cw[CONV_WIDTH - 1:CONV_WIDTH]
        for k in range(CONV_WIDTH - 1):
            off = SUBLANES - (CONV_WIDTH - 1) + k
            acc = acc + xa_buf[q, off:off + seg, :] * cw[k:k + 1]
        xc_parts.append(acc)
        xa_buf[q, 0:SUBLANES, :] = xa_buf[q, seg:seg + SUBLANES, :]
    xc = xc_parts[0] if nseq == 1 else jnp.concatenate(xc_parts, axis=0)
    convl_ref[...] = xa_buf[:, 0:SUBLANES, :]

    xcb = xc.astype(BF16)
    r_parts, i_parts = [], []
    for j in range(D_RNN // MXU_DIM):
        gj = jnp.dot(xcb[:, j * MXU_DIM:(j + 1) * MXU_DIM], wg_ref[j], preferred_element_type=F32)
        r_parts.append(gj[:, 0:MXU_DIM])
        i_parts.append(gj[:, MXU_DIM:2 * MXU_DIM])
    r = jax.nn.sigmoid(jnp.concatenate(r_parts, axis=1) + brg_ref[...])
    ig = jax.nn.sigmoid(jnp.concatenate(i_parts, axis=1) + big_ref[...])
    lam = lam_ref[...]
    softplus_neg = jnp.maximum(-lam, 0.0) + jnp.log1p(jnp.exp(-jnp.abs(lam)))
    log_a = (-LRU_C) * r * softplus_neg
    a = jnp.exp(log_a)
    mult = jnp.sqrt(jnp.maximum(1.0 - a * a, 0.0))
    if start_zero:
        row = lax.broadcasted_iota(I32, (rows, 1), 0)
        mult = jnp.where(jnp.logical_and(row == 0, s == 0), 1.0, mult)
    bterm = mult * ig * xc

    sub = lax.broadcasted_iota(I32, (SUBLANES, D_RNN), 0)
    h_parts = []
    carry = None
    groups_per_seq = seg // SUBLANES
    for g in range(rows // SUBLANES):
        q = g // groups_per_seq
        if g % groups_per_seq == 0:
            carry = h_carry[q:q + 1, :]
        ag = a[g * SUBLANES:(g + 1) * SUBLANES]
        bg = bterm[g * SUBLANES:(g + 1) * SUBLANES]
        d = 1
        while d < SUBLANES:
            keep = sub >= d
            a_sh = jnp.where(keep, pltpu.roll(ag, d, 0), 1.0)
            b_sh = jnp.where(keep, pltpu.roll(bg, d, 0), 0.0)
            bg = ag * b_sh + bg
            ag = ag * a_sh
            d *= 2
        hg = bg + ag * carry
        carry = hg[SUBLANES - 1:SUBLANES, :]
        h_parts.append(hg)
        if g % groups_per_seq == groups_per_seq - 1:
            h_carry[q:q + 1, :] = carry
    h = jnp.concatenate(h_parts, axis=0)
    hl_ref[...] = h_carry[...].reshape(nseq, 1, D_RNN)

    ya = jnp.dot((h * jax.nn.gelu(ga)).astype(BF16), wpa_ref[...], preferred_element_type=F32)

    u = jax.nn.gelu(zu)
    gv = jax.nn.gelu(zv)
    mu = jnp.mean(gv, axis=-1, keepdims=True)
    vc = gv - mu
    var = jnp.mean(vc * vc, axis=-1, keepdims=True)
    v = vc * lax.rsqrt(var + LN_EPS) * lng_ref[...] + lnb_ref[...]
    if want_v:
        v_ref[...] = v.reshape(nseq, seg, D_MLP)
    vb = v.astype(BF16)
    tri = lax.broadcasted_iota(I32, (mc, mc), 0) >= lax.broadcasted_iota(I32, (mc, mc), 1)
    ws_tri = [jnp.where(tri, ws_ref[g][0:mc, 0:mc], 0.0).astype(BF16) for g in range(N_MLP_GROUPS)]
    bst = bst_ref[...]
    mix_rows = []
    for c in range(rows // mc):
        cols = []
        for g in range(N_MLP_GROUPS):
            vg = vb[c * mc:(c + 1) * mc, g * MLP_GROUP_DIM:(g + 1) * MLP_GROUP_DIM]
            mg = jnp.dot(ws_tri[g], vg, preferred_element_type=F32) + bst[0:mc, g:g + 1]
            cols.append(mg)
        mix_rows.append(jnp.concatenate(cols, axis=1))
    vmix = mix_rows[0] if len(mix_rows) == 1 else jnp.concatenate(mix_rows, axis=0)
    yb = jnp.dot((u * vmix).astype(BF16), wpb_ref[...], preferred_element_type=F32)

    merged = jax.nn.sigmoid(gate_a) * ya + jax.nn.sigmoid(gate_b) * yb
    x1 = x + jnp.dot(merged.astype(BF16), wout_ref[...], preferred_element_type=F32)
    x1_ref[...] = x1.reshape(nseq, seg, D_MODEL)

    ms2 = jnp.mean(x1 * x1, axis=-1, keepdims=True)
    xn2 = x1 * lax.rsqrt(ms2 + RMS_EPS) * g2_ref[...]
    xn2_ref[...] = xn2.reshape(nseq, seg, D_MODEL)
    lgt = lax.dot_general(wrt_ref[...], xn2.astype(BF16), (((1,), (1,)), ((), ())), preferred_element_type=F32)
    lgt_ref[...] = lgt + brt_ref[...]


def _mixer(x, conv0, h0, wts, *, nseq, seg, start_zero, want_v):
    B, S, _ = x.shape
    assert B % nseq == 0 and S % seg == 0 and seg % SUBLANES == 0
    assert nseq == 1 or seg == S
    mc = min(seg, MLP_CHUNK)
    assert seg % mc == 0
    rows = nseq * seg
    n_s = S // seg
    grid = (B // nseq, n_s)
    seq_spec = lambda w: pl.BlockSpec((nseq, seg, w), lambda b, s: (b, s, 0))
    state_spec = lambda r: pl.BlockSpec((nseq, r, D_RNN), lambda b, s: (b, 0, 0))
    in_specs = [seq_spec(D_MODEL), state_spec(SUBLANES), state_spec(1)] + [_const_spec(w.shape) for w in wts]
    out_shape = [
        jax.ShapeDtypeStruct((B, S, D_MODEL), F32),
        jax.ShapeDtypeStruct((B, S, D_MODEL), F32),
        jax.ShapeDtypeStruct((N_EXPERTS, B * S), F32),
        jax.ShapeDtypeStruct((B, SUBLANES, D_RNN), F32),
        jax.ShapeDtypeStruct((B, 1, D_RNN), F32),
    ]
    out_specs = [
        seq_spec(D_MODEL), seq_spec(D_MODEL),
        pl.BlockSpec((N_EXPERTS, rows), lambda b, s: (0, b * n_s + s)),
        state_spec(SUBLANES), state_spec(1),
    ]
    if want_v:
        out_shape.append(jax.ShapeDtypeStruct((B, S, D_MLP), F32))
        out_specs.append(seq_spec(D_MLP))
    body = functools.partial(_mixer_body, nseq=nseq, seg=seg, mc=mc, start_zero=start_zero, want_v=want_v)
    return pl.pallas_call(
        body,
        grid=grid,
        in_specs=in_specs,
        out_specs=out_specs,
        out_shape=out_shape,
        scratch_shapes=[pltpu.VMEM((nseq, seg + SUBLANES, D_RNN), F32), pltpu.VMEM((nseq, D_RNN), F32)],
        compiler_params=pltpu.CompilerParams(dimension_semantics=("arbitrary", "arbitrary"),
                                             vmem_limit_bytes=VMEM_LIMIT_BYTES),
        name="mixer",
    )(x, conv0, h0, *wts)


def _route_body(lgt_ref, e_ref, gate_ref, dest_ref, blk_ref, cnt_ref, pstart_ref, nused_ref, *, n_tok, n_blk_pad):
    ch = ROUTE_CHUNK
    n_chunks = n_tok // ch
    eiota = lax.broadcasted_iota(I32, (N_EXPERTS, ch), 0)
    upper = (lax.broadcasted_iota(I32, (ch, ch), 0) < lax.broadcasted_iota(I32, (ch, ch), 1)).astype(BF16)

    def rank_chunk(c, counts):
        off = pl.multiple_of(c * ch, ch)
        work = lgt_ref[:, pl.ds(off, ch)]
        sel = jnp.zeros((N_EXPERTS, ch), F32)
        es, vs = [], []
        for _ in range(TOP_K):
            m = jnp.max(work, axis=0, keepdims=True)
            idx = jnp.min(jnp.where(work == m, eiota, N_EXPERTS), axis=0, keepdims=True)
            hit = eiota == idx
            es.append(idx)
            vs.append(m)
            work = jnp.where(hit, -jnp.inf, work)
            sel = jnp.where(hit, 1.0, sel)
        exps = [jnp.exp(v - vs[0]) for v in vs]
        inv = 1.0 / (exps[0] + exps[1] + exps[2] + exps[3])
        before = jnp.dot(sel.astype(BF16), upper, preferred_element_type=F32) + counts
        for k in range(TOP_K):
            rank = jnp.sum(jnp.where(eiota == es[k], before, 0.0), axis=0, keepdims=True)
            e_ref[k:k + 1, pl.ds(off, ch)] = es[k]
            gate_ref[k:k + 1, pl.ds(off, ch)] = exps[k] * inv
            dest_ref[k:k + 1, pl.ds(off, ch)] = rank.astype(I32)
        return counts + jnp.sum(sel, axis=1, keepdims=True)

    counts = lax.fori_loop(0, n_chunks, rank_chunk, jnp.zeros((N_EXPERTS, 1), F32)).astype(I32)
    shift = ROW_BLOCK.bit_length() - 1
    padded = lax.shift_left(lax.shift_right_logical(counts + (ROW_BLOCK - 1), shift), shift)
    col = lax.broadcasted_iota(I32, (N_EXPERTS, 1), 0)
    pad_end = jnp.zeros((N_EXPERTS, 1), I32)
    for e in range(N_EXPERTS):
        tot = jnp.sum(jnp.where(col <= e, padded, 0), axis=0, keepdims=True)
        pad_end = jnp.where(col == e, tot, pad_end)
    pad_start = pad_end - padded
    cnt_ref[...] = counts
    pstart_ref[...] = pad_start
    nused_ref[...] = lax.shift_right_logical(jnp.max(pad_end, axis=0, keepdims=True), shift)
    blk_row = lax.broadcasted_iota(I32, (N_EXPERTS, n_blk_pad), 1) * ROW_BLOCK
    blk_ref[...] = jnp.minimum(jnp.sum((pad_end <= blk_row).astype(I32), axis=0, keepdims=True), N_EXPERTS - 1)

    def place_chunk(c, carry):
        off = pl.multiple_of(c * ch, ch)
        for k in range(TOP_K):
            ek = e_ref[k:k + 1, pl.ds(off, ch)]
            base = jnp.sum(jnp.where(eiota == ek, pad_start, 0), axis=0, keepdims=True)
            dest_ref[k:k + 1, pl.ds(off, ch)] = dest_ref[k:k + 1, pl.ds(off, ch)] + base
        return carry

    lax.fori_loop(0, n_chunks, place_chunk, 0)


def _route(lgt, n_blk):
    n_tok = lgt.shape[1]
    assert n_tok % ROUTE_CHUNK == 0
    n_blk_pad = -(-n_blk // LANES) * LANES
    full = lambda shape: pl.BlockSpec(shape, lambda i: (0,) * len(shape))
    return pl.pallas_call(
        functools.partial(_route_body, n_tok=n_tok, n_blk_pad=n_blk_pad),
        grid=(1,),
        in_specs=[full((N_EXPERTS, n_tok))],
        out_specs=[full((TOP_K, n_tok)), full((TOP_K, n_tok)), full((TOP_K, n_tok)), full((1, n_blk_pad)),
                   full((N_EXPERTS, 1)), full((N_EXPERTS, 1)), full((1, 1))],
        out_shape=[jax.ShapeDtypeStruct((TOP_K, n_tok), I32),
                   jax.ShapeDtypeStruct((TOP_K, n_tok), F32),
                   jax.ShapeDtypeStruct((TOP_K, n_tok), I32),
                   jax.ShapeDtypeStruct((1, n_blk_pad), I32),
                   jax.ShapeDtypeStruct((N_EXPERTS, 1), I32),
                   jax.ShapeDtypeStruct((N_EXPERTS, 1), I32),
                   jax.ShapeDtypeStruct((1, 1), I32)],
        compiler_params=pltpu.CompilerParams(dimension_semantics=("arbitrary",), vmem_limit_bytes=VMEM_LIMIT_BYTES),
        name="route",
    )(lgt)


def _row_copy(src, src_row, dst, dst_row, sem):
    return pltpu.make_async_copy(src.at[pl.ds(src_row, 1), :], dst.at[pl.ds(dst_row, 1), :], sem)


def _dispatch_body(cnt_ref, pstart_ref, nused_ref, dest_ref, xp_ref, xs_ref, out_ref, zbuf, sem, *,
                   n_prompt_tiles, n_blk):
    i = pl.program_id(0)

    @pl.when(i == 0)
    def _():
        zbuf[...] = jnp.zeros_like(zbuf)
        for e in range(N_EXPERTS):
            cnt = cnt_ref[e]
            first = pstart_ref[e] + cnt
            n_pad = (-cnt) & (ROW_BLOCK - 1)

            def fill(j, c):
                _row_copy(zbuf, 0, out_ref, first + j, sem).start()
                return c

            def drain(j, c):
                _row_copy(zbuf, 0, out_ref, first + j, sem).wait()
                return c

            lax.fori_loop(0, n_pad, fill, 0)
            lax.fori_loop(0, n_pad, drain, 0)

        def fill_blk(b, c):
            cp = pltpu.make_async_copy(zbuf, out_ref.at[pl.ds(pl.multiple_of(b * ROW_BLOCK, ROW_BLOCK), ROW_BLOCK), :], sem)
            cp.start()
            cp.wait()
            return c

        lax.fori_loop(nused_ref[0], n_blk, fill_blk, 0)

    def scatter_rows(src):
        def issue(r, c):
            for k in range(TOP_K):
                _row_copy(src, r, out_ref, dest_ref[0, 0, k * TOK_TILE + r], sem).start()
            return c

        def drain(r, c):
            for k in range(TOP_K):
                _row_copy(src, r, out_ref, dest_ref[0, 0, k * TOK_TILE + r], sem).wait()
            return c

        lax.fori_loop(0, TOK_TILE, issue, 0)
        lax.fori_loop(0, TOK_TILE, drain, 0)

    @pl.when(i < n_prompt_tiles)
    def _():
        scatter_rows(xp_ref)

    @pl.when(i >= n_prompt_tiles)
    def _():
        scatter_rows(xs_ref)


def _dispatch(cnt, pstart, nused, dest_tiles, xn2_p, xn2_s, n_blk):
    n_p = xn2_p.shape[0] // TOK_TILE
    n_s = xn2_s.shape[0] // TOK_TILE
    assert xn2_p.shape[0] % TOK_TILE == 0 and xn2_s.shape[0] % TOK_TILE == 0
    return pl.pallas_call(
        functools.partial(_dispatch_body, n_prompt_tiles=n_p, n_blk=n_blk),
        grid_spec=pltpu.PrefetchScalarGridSpec(
            num_scalar_prefetch=3,
            grid=(n_p + n_s,),
            in_specs=[
                pl.BlockSpec((1, 1, TOP_K * TOK_TILE), lambda i, *_: (i, 0, 0), memory_space=pltpu.SMEM),
                pl.BlockSpec((TOK_TILE, D_MODEL), lambda i, *_: (jnp.minimum(i, n_p - 1), 0)),
                pl.BlockSpec((TOK_TILE, D_MODEL), lambda i, *_: (jnp.maximum(i - n_p, 0), 0)),
            ],
            out_specs=pl.BlockSpec(memory_space=pl.ANY),
            scratch_shapes=[pltpu.VMEM((ROW_BLOCK, D_MODEL), F32), pltpu.SemaphoreType.DMA],
        ),
        out_shape=jax.ShapeDtypeStruct((n_blk * ROW_BLOCK, D_MODEL), F32),
        compiler_params=pltpu.CompilerParams(dimension_semantics=("arbitrary",), vmem_limit_bytes=VMEM_LIMIT_BYTES),
        name="dispatch",
    )(cnt, pstart, nused, dest_tiles, xn2_p, xn2_s)


def _ffn_body(blk_ref, nused_ref, x_ref, wgu_ref, bgu_ref, wd_ref, bd_ref, o_ref):
    i = pl.program_id(0)

    @pl.when(i < nused_ref[0])
    def _():
        gu = jnp.dot(x_ref[...].astype(BF16), wgu_ref[0], preferred_element_type=F32) + bgu_ref[0]
        gate = jnp.minimum(gu[:, 0:D_EXPERT], SWIGLU_LIMIT)
        up = jnp.clip(gu[:, D_EXPERT:2 * D_EXPERT], -SWIGLU_LIMIT, SWIGLU_LIMIT)
        act = (up + 1.0) * (gate * jax.nn.sigmoid(SWIGLU_ALPHA * gate))
        o_ref[...] = jnp.dot(act.astype(BF16), wd_ref[0], preferred_element_type=F32) + bd_ref[0]

    @pl.when(i >= nused_ref[0])
    def _():
        o_ref[...] = jnp.zeros_like(o_ref)


def _ffn(blk_exp, nused, xs, wgu, bgu, wd, bd):
    n_blk = xs.shape[0] // ROW_BLOCK
    row_map = lambda i, blk, nu: (jnp.minimum(i, nu[0] - 1), 0)
    exp_map = lambda i, blk, nu: (blk[jnp.minimum(i, nu[0] - 1)], 0, 0)
    return pl.pallas_call(
        _ffn_body,
        grid_spec=pltpu.PrefetchScalarGridSpec(
            num_scalar_prefetch=2,
            grid=(n_blk,),
            in_specs=[
                pl.BlockSpec((ROW_BLOCK, D_MODEL), row_map),
                pl.BlockSpec((1, D_MODEL, 2 * D_EXPERT), exp_map),
                pl.BlockSpec((1, 1, 2 * D_EXPERT), exp_map),
                pl.BlockSpec((1, D_EXPERT, D_MODEL), exp_map),
                pl.BlockSpec((1, 1, D_MODEL), exp_map),
            ],
            out_specs=pl.BlockSpec((ROW_BLOCK, D_MODEL), lambda i, blk, nu: (i, 0)),
        ),
        out_shape=jax.ShapeDtypeStruct(xs.shape, F32),
        compiler_params=pltpu.CompilerParams(dimension_semantics=("arbitrary",), vmem_limit_bytes=VMEM_LIMIT_BYTES),
        name="ffn",
    )(blk_exp, nused, xs, wgu, bgu, wd, bd)


def _combine_body(dest_ref, ys_ref, x1_ref, gate_ref, gf_ref, y_ref, buf, sem):
    def issue(r, c):
        for k in range(TOP_K):
            pltpu.make_async_copy(ys_ref.at[pl.ds(dest_ref[0, 0, k * TOK_TILE + r], 1), :],
                                  buf.at[k, pl.ds(r, 1), :], sem).start()
        return c

    def drain(r, c):
        for k in range(TOP_K):
            pltpu.make_async_copy(ys_ref.at[pl.ds(dest_ref[0, 0, k * TOK_TILE + r], 1), :],
                                  buf.at[k, pl.ds(r, 1), :], sem).wait()
        return c

    lax.fori_loop(0, TOK_TILE, issue, 0)
    lax.fori_loop(0, TOK_TILE, drain, 0)
    gates = gate_ref[...]
    moe = buf[0] * gates[:, 0:1]
    for k in range(1, TOP_K):
        moe = moe + buf[k] * gates[:, k:k + 1]
    x2 = x1_ref[...] + moe
    ms = jnp.mean(x2 * x2, axis=-1, keepdims=True)
    y_ref[...] = x2 * lax.rsqrt(ms + RMS_EPS) * gf_ref[...]


def _combine(dest_tiles, ys, x1, gates_t, gf, tile0):
    n = x1.shape[0]
    assert n % TOK_TILE == 0
    return pl.pallas_call(
        _combine_body,
        grid=(n // TOK_TILE,),
        in_specs=[
            pl.BlockSpec((1, 1, TOP_K * TOK_TILE), lambda i: (tile0 + i, 0, 0), memory_space=pltpu.SMEM),
            pl.BlockSpec(memory_space=pl.ANY),
            pl.BlockSpec((TOK_TILE, D_MODEL), lambda i: (i, 0)),
            pl.BlockSpec((TOK_TILE, TOP_K), lambda i: (tile0 + i, 0)),
            pl.BlockSpec((1, D_MODEL), lambda i: (0, 0)),
        ],
        out_specs=pl.BlockSpec((TOK_TILE, D_MODEL), lambda i: (i, 0)),
        out_shape=jax.ShapeDtypeStruct((n, D_MODEL), F32),
        scratch_shapes=[pltpu.VMEM((TOP_K, TOK_TILE, D_MODEL), F32), pltpu.SemaphoreType.DMA],
        compiler_params=pltpu.CompilerParams(dimension_semantics=("arbitrary",), vmem_limit_bytes=VMEM_LIMIT_BYTES),
        name="combine",
    )(dest_tiles, ys, x1, gates_t, gf)


def _pack_gate_weights(w_rg, w_ig):
    def bd(w):
        w = w.reshape(D_RNN // MXU_DIM, GATE_PACK, LRU_BLOCK, LRU_BLOCK)
        eye = jnp.eye(GATE_PACK, dtype=w.dtype)
        return jnp.einsum('jpik,pq->jpiqk', w, eye).reshape(D_RNN // MXU_DIM, MXU_DIM, MXU_DIM)
    return jnp.concatenate([bd(w_rg), bd(w_ig)], axis=-1).astype(BF16)


def kernel(x_prompt, x_sample, state_conv, state_h, norm1_g, w_in, conv_w, conv_b, w_rg, b_rg, w_ig, b_ig, lru_lambda, ln_v_g, ln_v_b, w_s, b_s, w_proj_a, w_proj_b, w_out, norm2_g, w_router, b_router, w_gate_up, b_gate_up, w_down, b_down, final_norm_g):
    assert norm1_g.shape[0] == 1, "single layer"
    B, S, _ = x_prompt.shape
    Bs, Ss, _ = x_sample.shape
    row = lambda p: p.reshape(1, -1).astype(F32)
    wts = (
        row(norm1_g[0]), w_in[0].astype(BF16), conv_w[0], row(conv_b[0]), _pack_gate_weights(w_rg[0], w_ig[0]),
        row(b_rg[0]), row(b_ig[0]), row(lru_lambda[0]), row(ln_v_g[0]), row(ln_v_b[0]), w_s[0], b_s[0].T,
        w_proj_a[0].astype(BF16), w_proj_b[0].astype(BF16), w_out[0].astype(BF16), row(norm2_g[0]),
        w_router[0].T.astype(BF16), b_router[0].reshape(N_EXPERTS, 1),
    )
    x1_p, xn2_p, lgt_p, convl_p, hl_p = _mixer(
        x_prompt, jnp.zeros((B, SUBLANES, D_RNN), F32), jnp.zeros((B, 1, D_RNN), F32), wts,
        nseq=1, seg=min(MIX_ROWS, S), start_zero=True, want_v=False)
    conv0_s = jnp.pad(state_conv[0], ((0, 0), (SUBLANES - (CONV_WIDTH - 1), 0), (0, 0)))
    x1_s, xn2_s, lgt_s, convl_s, hl_s, v_s = _mixer(
        x_sample, conv0_s, state_h[0][:, None, :], wts,
        nseq=Bs, seg=Ss, start_zero=False, want_v=True)

    n_p, n_s = B * S, Bs * Ss
    n_tok = n_p + n_s
    n_blk = -(-(n_tok * TOP_K + N_EXPERTS * (ROW_BLOCK - 1)) // ROW_BLOCK)
    lgt = jnp.concatenate([lgt_p, lgt_s], axis=1)
    _, gate, dest, blk_exp, cnt, pstart, nused = _route(lgt, n_blk)
    n_tiles = n_tok // TOK_TILE
    dest_tiles = dest.reshape(TOP_K, n_tiles, TOK_TILE).transpose(1, 0, 2).reshape(n_tiles, 1, TOP_K * TOK_TILE)
    gates_t = gate.T
    cnt, pstart, nused, blk_exp = cnt.reshape(-1), pstart.reshape(-1), nused.reshape(-1), blk_exp.reshape(-1)

    xs = _dispatch(cnt, pstart, nused, dest_tiles, xn2_p.reshape(n_p, D_MODEL), xn2_s.reshape(n_s, D_MODEL), n_blk)
    ys = _ffn(blk_exp, nused, xs, w_gate_up[0].astype(BF16), b_gate_up[0][:, None, :], w_down[0].astype(BF16),
              b_down[0][:, None, :])
    gf = row(final_norm_g)
    y_p = _combine(dest_tiles, ys, x1_p.reshape(n_p, D_MODEL), gates_t, gf, 0).reshape(B, S, D_MODEL)
    y_s = _combine(dest_tiles, ys, x1_s.reshape(n_s, D_MODEL), gates_t, gf, n_p // TOK_TILE).reshape(Bs, Ss, D_MODEL)

    keep = slice(SUBLANES - (CONV_WIDTH - 1), SUBLANES)
    return (y_p, y_s, convl_p[None, :, keep, :], hl_p.reshape(1, B, D_RNN), convl_s[None, :, keep, :],
            hl_s.reshape(1, Bs, D_RNN), v_s[None])
```

```python
import functools

import jax
import jax.numpy as jnp
from jax import lax
from jax.experimental import pallas as pl
from jax.experimental.pallas import tpu as pltpu

F32 = jnp.float32
BF16 = jnp.bfloat16
I32 = jnp.int32

D_MODEL = 1024
D_RNN = D_MODEL
N_LRU_BLOCKS = 16
LRU_BLOCK = D_RNN // N_LRU_BLOCKS
CONV_WIDTH = 4
LRU_C = 8.0
MLP_CHUNK = 128
N_MLP_GROUPS = 4
MLP_GROUP_DIM = D_MODEL // 8
D_MLP = N_MLP_GROUPS * MLP_GROUP_DIM
N_IN = 2 * D_RNN + 2 * D_MLP + 2 * D_MODEL
N_EXPERTS = 32
TOP_K = 4
D_EXPERT = D_MODEL
SWIGLU_LIMIT = 7.0
SWIGLU_ALPHA = 1.702
RMS_EPS = 1e-6
LN_EPS = 1e-5
PAST_LEN = 1024

SUBLANES = 8
LANES = 128
MXU_DIM = 256
VMEM_LIMIT_BYTES = 56 * 1024 * 1024

MIX_ROWS = 256
ROUTE_CHUNK = 256
ROW_BLOCK = 256
TOK_TILE = 256
ISSUE_UNROLL = 8
GATE_PACK = MXU_DIM // LRU_BLOCK


def _const_spec(shape):
    nd = len(shape)
    return pl.BlockSpec(shape, lambda *_: (0,) * nd, pipeline_mode=pl.Buffered(1))


def _mixer_body(x_ref, conv0_ref, h0_ref, g1_ref, win_ref, convw_ref, convb_ref, wg_ref, brg_ref, big_ref,
                lam_ref, lng_ref, lnb_ref, ws_ref, bst_ref, wpa_ref, wpb_ref, wout_ref, g2_ref, wrt_ref,
                brt_ref, *rest, nseq, seg, mc, start_zero, want_v):
    if want_v:
        x1_ref, xn2_ref, lgt_ref, convl_ref, hl_ref, v_ref, xa_buf, h_carry = rest
    else:
        x1_ref, xn2_ref, lgt_ref, convl_ref, hl_ref, xa_buf, h_carry = rest
        v_ref = None
    s = pl.program_id(1)
    rows = nseq * seg

    @pl.when(s == 0)
    def _():
        xa_buf[:, 0:SUBLANES, :] = conv0_ref[...]
        h_carry[...] = h0_ref[...].reshape(nseq, D_RNN)

    x = x_ref[...].reshape(rows, D_MODEL)
    ms = jnp.mean(x * x, axis=-1, keepdims=True)
    xn = (x * lax.rsqrt(ms + RMS_EPS) * g1_ref[...]).astype(BF16)
    z = jnp.dot(xn, win_ref[...], preferred_element_type=F32)

    xa = z[:, 0:D_RNN]
    ga = z[:, D_RNN:2 * D_RNN]
    zu = z[:, 2 * D_RNN:2 * D_RNN + D_MLP]
    zv = z[:, 2 * D_RNN + D_MLP:2 * D_RNN + 2 * D_MLP]
    gate_a = z[:, 2 * D_RNN + 2 * D_MLP:2 * D_RNN + 2 * D_MLP + D_MODEL]
    gate_b = z[:, 2 * D_RNN + 2 * D_MLP + D_MODEL:N_IN]

    cw = convw_ref[...]
    xc_parts = []
    for q in range(nseq):
        xa_buf[q, SUBLANES:SUBLANES + seg, :] = xa[q * seg:(q + 1) * seg]
        acc = convb_ref[...] + xa[q * seg:(q + 1) * seg] * cw[CONV_WIDTH - 1:CONV_WIDTH]
        for k in range(CONV_WIDTH - 1):
            off = SUBLANES - (CONV_WIDTH - 1) + k
            acc = acc + xa_buf[q, off:off + seg, :] * cw[k:k + 1]
        xc_parts.append(acc)
        xa_buf[q, 0:SUBLANES, :] = xa_buf[q, seg:seg + SUBLANES, :]
    xc = xc_parts[0] if nseq == 1 else jnp.concatenate(xc_parts, axis=0)
    convl_ref[...] = xa_buf[:, 0:SUBLANES, :]

    xcb = xc.astype(BF16)
    r_parts, i_parts = [], []
    for j in range(D_RNN // MXU_DIM):
        gj = jnp.dot(xcb[:, j * MXU_DIM:(j + 1) * MXU_DIM], wg_ref[j], preferred_element_type=F32)
        r_parts.append(gj[:, 0:MXU_DIM])
        i_parts.append(gj[:, MXU_DIM:2 * MXU_DIM])
    r = jax.nn.sigmoid(jnp.concatenate(r_parts, axis=1) + brg_ref[...])
    ig = jax.nn.sigmoid(jnp.concatenate(i_parts, axis=1) + big_ref[...])
    lam = lam_ref[...]
    softplus_neg = jnp.maximum(-lam, 0.0) + jnp.log1p(jnp.exp(-jnp.abs(lam)))
    log_a = (-LRU_C) * r * softplus_neg
    a = jnp.exp(log_a)
    mult = jnp.sqrt(jnp.maximum(1.0 - a * a, 0.0))
    if start_zero:
        row = lax.broadcasted_iota(I32, (rows, 1), 0)
        mult = jnp.where(jnp.logical_and(row == 0, s == 0), 1.0, mult)
    bterm = mult * ig * xc

    sub = lax.broadcasted_iota(I32, (SUBLANES, D_RNN), 0)
    h_parts = []
    carry = None
    groups_per_seq = seg // SUBLANES
    for g in range(rows // SUBLANES):
        q = g // groups_per_seq
        if g % groups_per_seq == 0:
            carry = h_carry[q:q + 1, :]
        ag = a[g * SUBLANES:(g + 1) * SUBLANES]
        bg = bterm[g * SUBLANES:(g + 1) * SUBLANES]
        d = 1
        while d < SUBLANES:
            keep = sub >= d
            a_sh = jnp.where(keep, pltpu.roll(ag, d, 0), 1.0)
            b_sh = jnp.where(keep, pltpu.roll(bg, d, 0), 0.0)
            bg = ag * b_sh + bg
            ag = ag * a_sh
            d *= 2
        hg = bg + ag * carry
        carry = hg[SUBLANES - 1:SUBLANES, :]
        h_parts.append(hg)
        if g % groups_per_seq == groups_per_seq - 1:
            h_carry[q:q + 1, :] = carry
    h = jnp.concatenate(h_parts, axis=0)
    hl_ref[...] = h_carry[...].reshape(nseq, 1, D_RNN)

    ya = jnp.dot((h * jax.nn.gelu(ga)).astype(BF16), wpa_ref[...], preferred_element_type=F32)

    u = jax.nn.gelu(zu)
    gv = jax.nn.gelu(zv)
    mu = jnp.mean(gv, axis=-1, keepdims=True)
    vc = gv - mu
    var = jnp.mean(vc * vc, axis=-1, keepdims=True)
    v = vc * lax.rsqrt(var + LN_EPS) * lng_ref[...] + lnb_ref[...]
    if want_v:
        v_ref[...] = v.reshape(nseq, seg, D_MLP)
    vb = v.astype(BF16)
    tri = lax.broadcasted_iota(I32, (mc, mc), 0) >= lax.broadcasted_iota(I32, (mc, mc), 1)
    ws_tri = [jnp.where(tri, ws_ref[g][0:mc, 0:mc], 0.0).astype(BF16) for g in range(N_MLP_GROUPS)]
    bst = bst_ref[...]
    mix_rows = []
    for c in range(rows // mc):
        cols = []
        for g in range(N_MLP_GROUPS):
            vg = vb[c * mc:(c + 1) * mc, g * MLP_GROUP_DIM:(g + 1) * MLP_GROUP_DIM]
            mg = jnp.dot(ws_tri[g], vg, preferred_element_type=F32) + bst[0:mc, g:g + 1]
            cols.append(mg)
        mix_rows.append(jnp.concatenate(cols, axis=1))
    vmix = mix_rows[0] if len(mix_rows) == 1 else jnp.concatenate(mix_rows, axis=0)
    yb = jnp.dot((u * vmix).astype(BF16), wpb_ref[...], preferred_element_type=F32)

    merged = jax.nn.sigmoid(gate_a) * ya + jax.nn.sigmoid(gate_b) * yb
    x1 = x + jnp.dot(merged.astype(BF16), wout_ref[...], preferred_element_type=F32)
    x1_ref[...] = x1.reshape(nseq, seg, D_MODEL)

    ms2 = jnp.mean(x1 * x1, axis=-1, keepdims=True)
    xn2 = x1 * lax.rsqrt(ms2 + RMS_EPS) * g2_ref[...]
    xn2_ref[...] = xn2.reshape(nseq, seg, D_MODEL)
    lgt = lax.dot_general(wrt_ref[...], xn2.astype(BF16), (((1,), (1,)), ((), ())), preferred_element_type=F32)
    lgt_ref[...] = lgt + brt_ref[...]


def _mixer(x, conv0, h0, wts, *, nseq, seg, start_zero, want_v):
    B, S, _ = x.shape
    assert B % nseq == 0 and S % seg == 0 and seg % SUBLANES == 0
    assert nseq == 1 or seg == S
    mc = min(seg, MLP_CHUNK)
    assert seg % mc == 0
    rows = nseq * seg
    n_s = S // seg
    grid = (B // nseq, n_s)
    seq_spec = lambda w: pl.BlockSpec((nseq, seg, w), lambda b, s: (b, s, 0))
    state_spec = lambda r: pl.BlockSpec((nseq, r, D_RNN), lambda b, s: (b, 0, 0))
    in_specs = [seq_spec(D_MODEL), state_spec(SUBLANES), state_spec(1)] + [_const_spec(w.shape) for w in wts]
    out_shape = [
        jax.ShapeDtypeStruct((B, S, D_MODEL), F32),
        jax.ShapeDtypeStruct((B, S, D_MODEL), F32),
        jax.ShapeDtypeStruct((N_EXPERTS, B * S), F32),
        jax.ShapeDtypeStruct((B, SUBLANES, D_RNN), F32),
        jax.ShapeDtypeStruct((B, 1, D_RNN), F32),
    ]
    out_specs = [
        seq_spec(D_MODEL), seq_spec(D_MODEL),
        pl.BlockSpec((N_EXPERTS, rows), lambda b, s: (0, b * n_s + s)),
        state_spec(SUBLANES), state_spec(1),
    ]
    if want_v:
        out_shape.append(jax.ShapeDtypeStruct((B, S, D_MLP), F32))
        out_specs.append(seq_spec(D_MLP))
    body = functools.partial(_mixer_body, nseq=nseq, seg=seg, mc=mc, start_zero=start_zero, want_v=want_v)
    return pl.pallas_call(
        body,
        grid=grid,
        in_specs=in_specs,
        out_specs=out_specs,
        out_shape=out_shape,
        scratch_shapes=[pltpu.VMEM((nseq, seg + SUBLANES, D_RNN), F32), pltpu.VMEM((nseq, D_RNN), F32)],
        compiler_params=pltpu.CompilerParams(dimension_semantics=("arbitrary", "arbitrary"),
                                             vmem_limit_bytes=VMEM_LIMIT_BYTES),
        name="mixer",
    )(x, conv0, h0, *wts)


def _route_body(lgt_ref, e_ref, gate_ref, dest_ref, blk_ref, cnt_ref, pstart_ref, nused_ref, *, n_tok, n_blk_pad):
    ch = ROUTE_CHUNK
    n_chunks = n_tok // ch
    eiota = lax.broadcasted_iota(I32, (N_EXPERTS, ch), 0)
    upper = (lax.broadcasted_iota(I32, (ch, ch), 0) < lax.broadcasted_iota(I32, (ch, ch), 1)).astype(BF16)

    def rank_chunk(c, counts):
        off = pl.multiple_of(c * ch, ch)
        work = lgt_ref[:, pl.ds(off, ch)]
        sel = jnp.zeros((N_EXPERTS, ch), F32)
        es, vs = [], []
        for _ in range(TOP_K):
            m = jnp.max(work, axis=0, keepdims=True)
            idx = jnp.min(jnp.where(work == m, eiota, N_EXPERTS), axis=0, keepdims=True)
            hit = eiota == idx
            es.append(idx)
            vs.append(m)
            work = jnp.where(hit, -jnp.inf, work)
            sel = jnp.where(hit, 1.0, sel)
        exps = [jnp.exp(v - vs[0]) for v in vs]
        inv = 1.0 / (exps[0] + exps[1] + exps[2] + exps[3])
        before = jnp.dot(sel.astype(BF16), upper, preferred_element_type=F32) + counts
        for k in range(TOP_K):
            rank = jnp.sum(jnp.where(eiota == es[k], before, 0.0), axis=0, keepdims=True)
            e_ref[k:k + 1, pl.ds(off, ch)] = es[k]
            gate_ref[k:k + 1, pl.ds(off, ch)] = exps[k] * inv
            dest_ref[k:k + 1, pl.ds(off, ch)] = rank.astype(I32)
        return counts + jnp.sum(sel, axis=1, keepdims=True)

    counts = lax.fori_loop(0, n_chunks, rank_chunk, jnp.zeros((N_EXPERTS, 1), F32)).astype(I32)
    shift = ROW_BLOCK.bit_length() - 1
    padded = lax.shift_left(lax.shift_right_logical(counts + (ROW_BLOCK - 1), shift), shift)
    col = lax.broadcasted_iota(I32, (N_EXPERTS, 1), 0)
    pad_end = jnp.zeros((N_EXPERTS, 1), I32)
    for e in range(N_EXPERTS):
        tot = jnp.sum(jnp.where(col <= e, padded, 0), axis=0, keepdims=True)
        pad_end = jnp.where(col == e, tot, pad_end)
    pad_start = pad_end - padded
    cnt_ref[...] = counts
    pstart_ref[...] = pad_start
    nused_ref[...] = lax.shift_right_logical(jnp.max(pad_end, axis=0, keepdims=True), shift)
    blk_row = lax.broadcasted_iota(I32, (N_EXPERTS, n_blk_pad), 1) * ROW_BLOCK
    blk_ref[...] = jnp.minimum(jnp.sum((pad_end <= blk_row).astype(I32), axis=0, keepdims=True), N_EXPERTS - 1)

    def place_chunk(c, carry):
        off = pl.multiple_of(c * ch, ch)
        for k in range(TOP_K):
            ek = e_ref[k:k + 1, pl.ds(off, ch)]
            base = jnp.sum(jnp.where(eiota == ek, pad_start, 0), axis=0, keepdims=True)
            dest_ref[k:k + 1, pl.ds(off, ch)] = dest_ref[k:k + 1, pl.ds(off, ch)] + base
        return carry

    lax.fori_loop(0, n_chunks, place_chunk, 0)


def _route(lgt, n_blk):
    n_tok = lgt.shape[1]
    assert n_tok % ROUTE_CHUNK == 0
    n_blk_pad = -(-n_blk // LANES) * LANES
    full = lambda shape: pl.BlockSpec(shape, lambda i: (0,) * len(shape))
    return pl.pallas_call(
        functools.partial(_route_body, n_tok=n_tok, n_blk_pad=n_blk_pad),
        grid=(1,),
        in_specs=[full((N_EXPERTS, n_tok))],
        out_specs=[full((TOP_K, n_tok)), full((TOP_K, n_tok)), full((TOP_K, n_tok)), full((1, n_blk_pad)),
                   full((N_EXPERTS, 1)), full((N_EXPERTS, 1)), full((1, 1))],
        out_shape=[jax.ShapeDtypeStruct((TOP_K, n_tok), I32),
                   jax.ShapeDtypeStruct((TOP_K, n_tok), F32),
                   jax.ShapeDtypeStruct((TOP_K, n_tok), I32),
                   jax.ShapeDtypeStruct((1, n_blk_pad), I32),
                   jax.ShapeDtypeStruct((N_EXPERTS, 1), I32),
                   jax.ShapeDtypeStruct((N_EXPERTS, 1), I32),
                   jax.ShapeDtypeStruct((1, 1), I32)],
        compiler_params=pltpu.CompilerParams(dimension_semantics=("arbitrary",), vmem_limit_bytes=VMEM_LIMIT_BYTES),
        name="route",
    )(lgt)


def _row_copy(src, src_row, dst, dst_row, sem):
    return pltpu.make_async_copy(src.at[pl.ds(src_row, 1), :], dst.at[pl.ds(dst_row, 1), :], sem)


def _dispatch_body(cnt_ref, pstart_ref, nused_ref, dest_ref, xp_ref, xs_ref, out_ref, zbuf, sem, *,
                   n_prompt_tiles, n_blk):
    i = pl.program_id(0)

    @pl.when(i == 0)
    def _():
        zbuf[...] = jnp.zeros_like(zbuf)
        for e in range(N_EXPERTS):
            cnt = cnt_ref[e]
            first = pstart_ref[e] + cnt
            n_pad = (-cnt) & (ROW_BLOCK - 1)

            def fill(j, c):
                _row_copy(zbuf, 0, out_ref, first + j, sem).start()
                return c

            def drain(j, c):
                _row_copy(zbuf, 0, out_ref, first + j, sem).wait()
                return c

            lax.fori_loop(0, n_pad, fill, 0)
            lax.fori_loop(0, n_pad, drain, 0)

        def fill_blk(b, c):
            cp = pltpu.make_async_copy(zbuf, out_ref.at[pl.ds(pl.multiple_of(b * ROW_BLOCK, ROW_BLOCK), ROW_BLOCK), :], sem)
            cp.start()
            cp.wait()
            return c

        lax.fori_loop(nused_ref[0], n_blk, fill_blk, 0)

    def scatter_rows(src):
        def issue(g, c):
            for u in range(ISSUE_UNROLL):
                r = g * ISSUE_UNROLL + u
                for k in range(TOP_K):
                    _row_copy(src, r, out_ref, dest_ref[0, 0, k * TOK_TILE + r], sem).start(priority=k % 2)
            return c

        lax.fori_loop(0, TOK_TILE // ISSUE_UNROLL, issue, 0)
        for k in range(TOP_K):
            pltpu.make_async_copy(src, out_ref.at[pl.ds(0, TOK_TILE), :], sem).wait()

    @pl.when(i < n_prompt_tiles)
    def _():
        scatter_rows(xp_ref)

    @pl.when(i >= n_prompt_tiles)
    def _():
        scatter_rows(xs_ref)


def _dispatch(cnt, pstart, nused, dest_tiles, xn2_p, xn2_s, n_blk):
    n_p = xn2_p.shape[0] // TOK_TILE
    n_s = xn2_s.shape[0] // TOK_TILE
    assert xn2_p.shape[0] % TOK_TILE == 0 and xn2_s.shape[0] % TOK_TILE == 0
    return pl.pallas_call(
        functools.partial(_dispatch_body, n_prompt_tiles=n_p, n_blk=n_blk),
        grid_spec=pltpu.PrefetchScalarGridSpec(
            num_scalar_prefetch=3,
            grid=(n_p + n_s,),
            in_specs=[
                pl.BlockSpec((1, 1, TOP_K * TOK_TILE), lambda i, *_: (i, 0, 0), memory_space=pltpu.SMEM),
                pl.BlockSpec((TOK_TILE, D_MODEL), lambda i, *_: (jnp.minimum(i, n_p - 1), 0)),
                pl.BlockSpec((TOK_TILE, D_MODEL), lambda i, *_: (jnp.maximum(i - n_p, 0), 0)),
            ],
            out_specs=pl.BlockSpec(memory_space=pl.ANY),
            scratch_shapes=[pltpu.VMEM((ROW_BLOCK, D_MODEL), F32), pltpu.SemaphoreType.DMA],
        ),
        out_shape=jax.ShapeDtypeStruct((n_blk * ROW_BLOCK, D_MODEL), F32),
        compiler_params=pltpu.CompilerParams(dimension_semantics=("arbitrary",), vmem_limit_bytes=VMEM_LIMIT_BYTES),
        name="dispatch",
    )(cnt, pstart, nused, dest_tiles, xn2_p, xn2_s)


def _ffn_body(blk_ref, nused_ref, x_ref, wgu_ref, bgu_ref, wd_ref, bd_ref, o_ref):
    i = pl.program_id(0)

    @pl.when(i < nused_ref[0])
    def _():
        gu = jnp.dot(x_ref[...].astype(BF16), wgu_ref[0], preferred_element_type=F32) + bgu_ref[0]
        gate = jnp.minimum(gu[:, 0:D_EXPERT], SWIGLU_LIMIT)
        up = jnp.clip(gu[:, D_EXPERT:2 * D_EXPERT], -SWIGLU_LIMIT, SWIGLU_LIMIT)
        act = (up + 1.0) * (gate * jax.nn.sigmoid(SWIGLU_ALPHA * gate))
        o_ref[...] = jnp.dot(act.astype(BF16), wd_ref[0], preferred_element_type=F32) + bd_ref[0]

    @pl.when(i >= nused_ref[0])
    def _():
        o_ref[...] = jnp.zeros_like(o_ref)


def _ffn(blk_exp, nused, xs, wgu, bgu, wd, bd):
    n_blk = xs.shape[0] // ROW_BLOCK
    row_map = lambda i, blk, nu: (jnp.minimum(i, nu[0] - 1), 0)
    exp_map = lambda i, blk, nu: (blk[jnp.minimum(i, nu[0] - 1)], 0, 0)
    return pl.pallas_call(
        _ffn_body,
        grid_spec=pltpu.PrefetchScalarGridSpec(
            num_scalar_prefetch=2,
            grid=(n_blk,),
            in_specs=[
                pl.BlockSpec((ROW_BLOCK, D_MODEL), row_map),
                pl.BlockSpec((1, D_MODEL, 2 * D_EXPERT), exp_map),
                pl.BlockSpec((1, 1, 2 * D_EXPERT), exp_map),
                pl.BlockSpec((1, D_EXPERT, D_MODEL), exp_map),
                pl.BlockSpec((1, 1, D_MODEL), exp_map),
            ],
            out_specs=pl.BlockSpec((ROW_BLOCK, D_MODEL), lambda i, blk, nu: (i, 0)),
        ),
        out_shape=jax.ShapeDtypeStruct(xs.shape, F32),
        compiler_params=pltpu.CompilerParams(dimension_semantics=("arbitrary",), vmem_limit_bytes=VMEM_LIMIT_BYTES),
        name="ffn",
    )(blk_exp, nused, xs, wgu, bgu, wd, bd)


def _combine_body(dest_ref, ys_ref, x1_ref, gate_ref, gf_ref, y_ref, buf, sem):
    def issue(g, c):
        for u in range(ISSUE_UNROLL):
            r = g * ISSUE_UNROLL + u
            for k in range(TOP_K):
                pltpu.make_async_copy(ys_ref.at[pl.ds(dest_ref[0, 0, k * TOK_TILE + r], 1), :],
                                      buf.at[k, pl.ds(r, 1), :], sem).start(priority=k % 2)
        return c

    lax.fori_loop(0, TOK_TILE // ISSUE_UNROLL, issue, 0)
    for k in range(TOP_K):
        pltpu.make_async_copy(ys_ref.at[pl.ds(0, TOK_TILE), :], buf.at[k], sem).wait()
    gates = gate_ref[...]
    moe = buf[0] * gates[:, 0:1]
    for k in range(1, TOP_K):
        moe = moe + buf[k] * gates[:, k:k + 1]
    x2 = x1_ref[...] + moe
    ms = jnp.mean(x2 * x2, axis=-1, keepdims=True)
    y_ref[...] = x2 * lax.rsqrt(ms + RMS_EPS) * gf_ref[...]


def _combine(dest_tiles, ys, x1, gates_t, gf, tile0):
    n = x1.shape[0]
    assert n % TOK_TILE == 0
    return pl.pallas_call(
        _combine_body,
        grid=(n // TOK_TILE,),
        in_specs=[
            pl.BlockSpec((1, 1, TOP_K * TOK_TILE), lambda i: (tile0 + i, 0, 0), memory_space=pltpu.SMEM),
            pl.BlockSpec(memory_space=pl.ANY),
            pl.BlockSpec((TOK_TILE, D_MODEL), lambda i: (i, 0)),
            pl.BlockSpec((TOK_TILE, TOP_K), lambda i: (tile0 + i, 0)),
            pl.BlockSpec((1, D_MODEL), lambda i: (0, 0)),
        ],
        out_specs=pl.BlockSpec((TOK_TILE, D_MODEL), lambda i: (i, 0)),
        out_shape=jax.ShapeDtypeStruct((n, D_MODEL), F32),
        scratch_shapes=[pltpu.VMEM((TOP_K, TOK_TILE, D_MODEL), F32), pltpu.SemaphoreType.DMA],
        compiler_params=pltpu.CompilerParams(dimension_semantics=("arbitrary",), vmem_limit_bytes=VMEM_LIMIT_BYTES),
        name="combine",
    )(dest_tiles, ys, x1, gates_t, gf)


def _pack_gate_weights(w_rg, w_ig):
    def bd(w):
        w = w.reshape(D_RNN // MXU_DIM, GATE_PACK, LRU_BLOCK, LRU_BLOCK)
        eye = jnp.eye(GATE_PACK, dtype=w.dtype)
        return jnp.einsum('jpik,pq->jpiqk', w, eye).reshape(D_RNN // MXU_DIM, MXU_DIM, MXU_DIM)
    return jnp.concatenate([bd(w_rg), bd(w_ig)], axis=-1).astype(BF16)


def kernel(x_prompt, x_sample, state_conv, state_h, norm1_g, w_in, conv_w, conv_b, w_rg, b_rg, w_ig, b_ig, lru_lambda, ln_v_g, ln_v_b, w_s, b_s, w_proj_a, w_proj_b, w_out, norm2_g, w_router, b_router, w_gate_up, b_gate_up, w_down, b_down, final_norm_g):
    assert norm1_g.shape[0] == 1, "single layer"
    B, S, _ = x_prompt.shape
    Bs, Ss, _ = x_sample.shape
    row = lambda p: p.reshape(1, -1).astype(F32)
    wts = (
        row(norm1_g[0]), w_in[0].astype(BF16), conv_w[0], row(conv_b[0]), _pack_gate_weights(w_rg[0], w_ig[0]),
        row(b_rg[0]), row(b_ig[0]), row(lru_lambda[0]), row(ln_v_g[0]), row(ln_v_b[0]), w_s[0], b_s[0].T,
        w_proj_a[0].astype(BF16), w_proj_b[0].astype(BF16), w_out[0].astype(BF16), row(norm2_g[0]),
        w_router[0].T.astype(BF16), b_router[0].reshape(N_EXPERTS, 1),
    )
    x1_p, xn2_p, lgt_p, convl_p, hl_p = _mixer(
        x_prompt, jnp.zeros((B, SUBLANES, D_RNN), F32), jnp.zeros((B, 1, D_RNN), F32), wts,
        nseq=1, seg=min(MIX_ROWS, S), start_zero=True, want_v=False)
    conv0_s = jnp.pad(state_conv[0], ((0, 0), (SUBLANES - (CONV_WIDTH - 1), 0), (0, 0)))
    x1_s, xn2_s, lgt_s, convl_s, hl_s, v_s = _mixer(
        x_sample, conv0_s, state_h[0][:, None, :], wts,
        nseq=Bs, seg=Ss, start_zero=False, want_v=True)

    n_p, n_s = B * S, Bs * Ss
    n_tok = n_p + n_s
    n_blk = -(-(n_tok * TOP_K + N_EXPERTS * (ROW_BLOCK - 1)) // ROW_BLOCK)
    lgt = jnp.concatenate([lgt_p, lgt_s], axis=1)
    _, gate, dest, blk_exp, cnt, pstart, nused = _route(lgt, n_blk)
    n_tiles = n_tok // TOK_TILE
    dest_tiles = dest.reshape(TOP_K, n_tiles, TOK_TILE).transpose(1, 0, 2).reshape(n_tiles, 1, TOP_K * TOK_TILE)
    gates_t = gate.T
    cnt, pstart, nused, blk_exp = cnt.reshape(-1), pstart.reshape(-1), nused.reshape(-1), blk_exp.reshape(-1)

    xs = _dispatch(cnt, pstart, nused, dest_tiles, xn2_p.reshape(n_p, D_MODEL), xn2_s.reshape(n_s, D_MODEL), n_blk)
    ys = _ffn(blk_exp, nused, xs, w_gate_up[0].astype(BF16), b_gate_up[0][:, None, :], w_down[0].astype(BF16),
              b_down[0][:, None, :])
    gf = row(final_norm_g)
    y_p = _combine(dest_tiles, ys, x1_p.reshape(n_p, D_MODEL), gates_t, gf, 0).reshape(B, S, D_MODEL)
    y_s = _combine(dest_tiles, ys, x1_s.reshape(n_s, D_MODEL), gates_t, gf, n_p // TOK_TILE).reshape(Bs, Ss, D_MODEL)

    keep = slice(SUBLANES - (CONV_WIDTH - 1), SUBLANES)
    return (y_p, y_s, convl_p[None, :, keep, :], hl_p.reshape(1, B, D_RNN), convl_s[None, :, keep, :],
            hl_s.reshape(1, Bs, D_RNN), v_s[None])
```

```python
import functools

import jax
import jax.numpy as jnp
from jax import lax
from jax.experimental import pallas as pl
from jax.experimental.pallas import tpu as pltpu

F32 = jnp.float32
BF16 = jnp.bfloat16
I32 = jnp.int32

D_MODEL = 1024
D_RNN = D_MODEL
N_LRU_BLOCKS = 16
LRU_BLOCK = D_RNN // N_LRU_BLOCKS
CONV_WIDTH = 4
LRU_C = 8.0
MLP_CHUNK = 128
N_MLP_GROUPS = 4
MLP_GROUP_DIM = D_MODEL // 8
D_MLP = N_MLP_GROUPS * MLP_GROUP_DIM
N_IN = 2 * D_RNN + 2 * D_MLP + 2 * D_MODEL
N_EXPERTS = 32
TOP_K = 4
D_EXPERT = D_MODEL
SWIGLU_LIMIT = 7.0
SWIGLU_ALPHA = 1.702
RMS_EPS = 1e-6
LN_EPS = 1e-5
PAST_LEN = 1024

SUBLANES = 8
LANES = 128
MXU_DIM = 256
ROW_TILE = D_MODEL // LANES
VMEM_LIMIT_BYTES = 56 * 1024 * 1024

MIX_ROWS = 256
ROUTE_CHUNK = 256
ROW_BLOCK = 256
TOK_TILE = 256
ISSUE_UNROLL = 8
GATE_PACK = MXU_DIM // LRU_BLOCK


def _const_spec(shape):
    nd = len(shape)
    return pl.BlockSpec(shape, lambda *_: (0,) * nd, pipeline_mode=pl.Buffered(1))


def _store_row_tiles(ref, val, n):
    for j in range(ROW_TILE):
        ref[pl.ds(j, n, stride=ROW_TILE), :] = val[:, j * LANES:(j + 1) * LANES]


def _load_row_tiles(ref, n):
    return jnp.concatenate([ref[pl.ds(j, n, stride=ROW_TILE), :] for j in range(ROW_TILE)], axis=1)


def _row_tile(ref, row):
    start = row * ROW_TILE
    if not isinstance(start, int):
        start = pl.multiple_of(start, ROW_TILE)
    return ref.at[pl.ds(start, ROW_TILE), :]


def _tile_copy(src, src_row, dst, dst_row, sem):
    return pltpu.make_async_copy(_row_tile(src, src_row), _row_tile(dst, dst_row), sem)


def _mixer_body(x_ref, conv0_ref, h0_ref, g1_ref, win_ref, convw_ref, convb_ref, wg_ref, brg_ref, big_ref,
                lam_ref, lng_ref, lnb_ref, ws_ref, bst_ref, wpa_ref, wpb_ref, wout_ref, g2_ref, wrt_ref,
                brt_ref, *rest, nseq, seg, mc, start_zero, want_v):
    if want_v:
        x1_ref, xn2_ref, lgt_ref, convl_ref, hl_ref, v_ref, xa_buf, h_carry = rest
    else:
        x1_ref, xn2_ref, lgt_ref, convl_ref, hl_ref, xa_buf, h_carry = rest
        v_ref = None
    s = pl.program_id(1)
    rows = nseq * seg

    @pl.when(s == 0)
    def _():
        xa_buf[:, 0:SUBLANES, :] = conv0_ref[...]
        h_carry[...] = h0_ref[...].reshape(nseq, D_RNN)

    x = x_ref[...].reshape(rows, D_MODEL)
    ms = jnp.mean(x * x, axis=-1, keepdims=True)
    xn = (x * lax.rsqrt(ms + RMS_EPS) * g1_ref[...]).astype(BF16)
    z = jnp.dot(xn, win_ref[...], preferred_element_type=F32)

    xa = z[:, 0:D_RNN]
    ga = z[:, D_RNN:2 * D_RNN]
    zu = z[:, 2 * D_RNN:2 * D_RNN + D_MLP]
    zv = z[:, 2 * D_RNN + D_MLP:2 * D_RNN + 2 * D_MLP]
    gate_a = z[:, 2 * D_RNN + 2 * D_MLP:2 * D_RNN + 2 * D_MLP + D_MODEL]
    gate_b = z[:, 2 * D_RNN + 2 * D_MLP + D_MODEL:N_IN]

    cw = convw_ref[...]
    xc_parts = []
    for q in range(nseq):
        xa_buf[q, SUBLANES:SUBLANES + seg, :] = xa[q * seg:(q + 1) * seg]
        acc = convb_ref[...] + xa[q * seg:(q + 1) * seg] * cw[CONV_WIDTH - 1:CONV_WIDTH]
        for k in range(CONV_WIDTH - 1):
            off = SUBLANES - (CONV_WIDTH - 1) + k
            acc = acc + xa_buf[q, off:off + seg, :] * cw[k:k + 1]
        xc_parts.append(acc)
        xa_buf[q, 0:SUBLANES, :] = xa_buf[q, seg:seg + SUBLANES, :]
    xc = xc_parts[0] if nseq == 1 else jnp.concatenate(xc_parts, axis=0)
    convl_ref[...] = xa_buf[:, 0:SUBLANES, :]

    xcb = xc.astype(BF16)
    r_parts, i_parts = [], []
    for j in range(D_RNN // MXU_DIM):
        gj = jnp.dot(xcb[:, j * MXU_DIM:(j + 1) * MXU_DIM], wg_ref[j], preferred_element_type=F32)
        r_parts.append(gj[:, 0:MXU_DIM])
        i_parts.append(gj[:, MXU_DIM:2 * MXU_DIM])
    r = jax.nn.sigmoid(jnp.concatenate(r_parts, axis=1) + brg_ref[...])
    ig = jax.nn.sigmoid(jnp.concatenate(i_parts, axis=1) + big_ref[...])
    lam = lam_ref[...]
    softplus_neg = jnp.maximum(-lam, 0.0) + jnp.log1p(jnp.exp(-jnp.abs(lam)))
    log_a = (-LRU_C) * r * softplus_neg
    a = jnp.exp(log_a)
    mult = jnp.sqrt(jnp.maximum(1.0 - a * a, 0.0))
    if start_zero:
        row = lax.broadcasted_iota(I32, (rows, 1), 0)
        mult = jnp.where(jnp.logical_and(row == 0, s == 0), 1.0, mult)
    bterm = mult * ig * xc

    sub = lax.broadcasted_iota(I32, (SUBLANES, D_RNN), 0)
    h_parts = []
    carry = None
    groups_per_seq = seg // SUBLANES
    for g in range(rows // SUBLANES):
        q = g // groups_per_seq
        if g % groups_per_seq == 0:
            carry = h_carry[q:q + 1, :]
        ag = a[g * SUBLANES:(g + 1) * SUBLANES]
        bg = bterm[g * SUBLANES:(g + 1) * SUBLANES]
        d = 1
        while d < SUBLANES:
            keep = sub >= d
            a_sh = jnp.where(keep, pltpu.roll(ag, d, 0), 1.0)
            b_sh = jnp.where(keep, pltpu.roll(bg, d, 0), 0.0)
            bg = ag * b_sh + bg
            ag = ag * a_sh
            d *= 2
        hg = bg + ag * carry
        carry = hg[SUBLANES - 1:SUBLANES, :]
        h_parts.append(hg)
        if g % groups_per_seq == groups_per_seq - 1:
            h_carry[q:q + 1, :] = carry
    h = jnp.concatenate(h_parts, axis=0)
    hl_ref[...] = h_carry[...].reshape(nseq, 1, D_RNN)

    ya = jnp.dot((h * jax.nn.gelu(ga)).astype(BF16), wpa_ref[...], preferred_element_type=F32)

    u = jax.nn.gelu(zu)
    gv = jax.nn.gelu(zv)
    mu = jnp.mean(gv, axis=-1, keepdims=True)
    vc = gv - mu
    var = jnp.mean(vc * vc, axis=-1, keepdims=True)
    v = vc * lax.rsqrt(var + LN_EPS) * lng_ref[...] + lnb_ref[...]
    if want_v:
        v_ref[...] = v.reshape(nseq, seg, D_MLP)
    vb = v.astype(BF16)
    tri = lax.broadcasted_iota(I32, (mc, mc), 0) >= lax.broadcasted_iota(I32, (mc, mc), 1)
    ws_tri = [jnp.where(tri, ws_ref[g][0:mc, 0:mc], 0.0).astype(BF16) for g in range(N_MLP_GROUPS)]
    bst = bst_ref[...]
    mix_rows = []
    for c in range(rows // mc):
        cols = []
        for g in range(N_MLP_GROUPS):
            vg = vb[c * mc:(c + 1) * mc, g * MLP_GROUP_DIM:(g + 1) * MLP_GROUP_DIM]
            mg = jnp.dot(ws_tri[g], vg, preferred_element_type=F32) + bst[0:mc, g:g + 1]
            cols.append(mg)
        mix_rows.append(jnp.concatenate(cols, axis=1))
    vmix = mix_rows[0] if len(mix_rows) == 1 else jnp.concatenate(mix_rows, axis=0)
    yb = jnp.dot((u * vmix).astype(BF16), wpb_ref[...], preferred_element_type=F32)

    merged = jax.nn.sigmoid(gate_a) * ya + jax.nn.sigmoid(gate_b) * yb
    x1 = x + jnp.dot(merged.astype(BF16), wout_ref[...], preferred_element_type=F32)
    x1_ref[...] = x1.reshape(nseq, seg, D_MODEL)

    ms2 = jnp.mean(x1 * x1, axis=-1, keepdims=True)
    xn2 = x1 * lax.rsqrt(ms2 + RMS_EPS) * g2_ref[...]
    _store_row_tiles(xn2_ref, xn2, rows)
    lgt =lax.dot_general(wrt_ref[...], xn2.astype(BF16), (((1,), (1,)), ((), ())), preferred_element_type=F32)
    lgt_ref[...] = lgt + brt_ref[...]


def _mixer(x, conv0, h0, wts, *, nseq, seg, start_zero, want_v):
    B, S, _ = x.shape
    assert B % nseq == 0 and S % seg == 0 and seg % SUBLANES == 0
    assert nseq == 1 or seg == S
    mc = min(seg, MLP_CHUNK)
    assert seg % mc == 0
    rows = nseq * seg
    n_s = S // seg
    grid = (B // nseq, n_s)
    seq_spec = lambda w: pl.BlockSpec((nseq, seg, w), lambda b, s: (b, s, 0))
    state_spec = lambda r: pl.BlockSpec((nseq, r, D_RNN), lambda b, s: (b, 0, 0))
    in_specs = [seq_spec(D_MODEL), state_spec(SUBLANES), state_spec(1)] + [_const_spec(w.shape) for w in wts]
    out_shape = [
        jax.ShapeDtypeStruct((B, S, D_MODEL), F32),
        jax.ShapeDtypeStruct((B * S * ROW_TILE, LANES), F32),
        jax.ShapeDtypeStruct((N_EXPERTS, B * S), F32),
        jax.ShapeDtypeStruct((B, SUBLANES, D_RNN), F32),
        jax.ShapeDtypeStruct((B, 1, D_RNN), F32),
    ]
    out_specs = [
        seq_spec(D_MODEL), pl.BlockSpec((rows * ROW_TILE, LANES), lambda b, s: (b * n_s + s, 0)),
        pl.BlockSpec((N_EXPERTS, rows), lambda b, s: (0, b * n_s + s)),
        state_spec(SUBLANES), state_spec(1),
    ]
    if want_v:
        out_shape.append(jax.ShapeDtypeStruct((B, S, D_MLP), F32))
        out_specs.append(seq_spec(D_MLP))
    body = functools.partial(_mixer_body, nseq=nseq, seg=seg, mc=mc, start_zero=start_zero, want_v=want_v)
    return pl.pallas_call(
        body,
        grid=grid,
        in_specs=in_specs,
        out_specs=out_specs,
        out_shape=out_shape,
        scratch_shapes=[pltpu.VMEM((nseq, seg + SUBLANES, D_RNN), F32), pltpu.VMEM((nseq, D_RNN), F32)],
        compiler_params=pltpu.CompilerParams(dimension_semantics=("arbitrary", "arbitrary"),
                                             vmem_limit_bytes=VMEM_LIMIT_BYTES),
        name="mixer",
    )(x, conv0, h0, *wts)


def _route_body(lgt_ref, e_ref, gate_ref, dest_ref, blk_ref, cnt_ref, pstart_ref, nused_ref, *, n_tok, n_blk_pad):
    ch = ROUTE_CHUNK
    n_chunks = n_tok // ch
    eiota = lax.broadcasted_iota(I32, (N_EXPERTS, ch), 0)
    upper = (lax.broadcasted_iota(I32, (ch, ch), 0) < lax.broadcasted_iota(I32, (ch, ch), 1)).astype(BF16)

    def rank_chunk(c, counts):
        off = pl.multiple_of(c * ch, ch)
        work = lgt_ref[:, pl.ds(off, ch)]
        sel = jnp.zeros((N_EXPERTS, ch), F32)
        es, vs = [], []
        for _ in range(TOP_K):
            m = jnp.max(work, axis=0, keepdims=True)
            idx = jnp.min(jnp.where(work == m, eiota, N_EXPERTS), axis=0, keepdims=True)
            hit = eiota == idx
            es.append(idx)
            vs.append(m)
            work = jnp.where(hit, -jnp.inf, work)
            sel = jnp.where(hit, 1.0, sel)
        exps = [jnp.exp(v - vs[0]) for v in vs]
        inv = 1.0 / (exps[0] + exps[1] + exps[2] + exps[3])
        before = jnp.dot(sel.astype(BF16), upper, preferred_element_type=F32) + counts
        for k in range(TOP_K):
            rank = jnp.sum(jnp.where(eiota == es[k], before, 0.0), axis=0, keepdims=True)
            e_ref[k:k + 1, pl.ds(off, ch)] = es[k]
            gate_ref[k:k + 1, pl.ds(off, ch)] = exps[k] * inv
            dest_ref[k:k + 1, pl.ds(off, ch)] = rank.astype(I32)
        return counts + jnp.sum(sel, axis=1, keepdims=True)

    counts = lax.fori_loop(0, n_chunks, rank_chunk, jnp.zeros((N_EXPERTS, 1), F32)).astype(I32)
    shift = ROW_BLOCK.bit_length() - 1
    padded = lax.shift_left(lax.shift_right_logical(counts + (ROW_BLOCK - 1), shift), shift)
    col = lax.broadcasted_iota(I32, (N_EXPERTS, 1), 0)
    pad_end = jnp.zeros((N_EXPERTS, 1), I32)
    for e in range(N_EXPERTS):
        tot = jnp.sum(jnp.where(col <= e, padded, 0), axis=0, keepdims=True)
        pad_end = jnp.where(col == e, tot, pad_end)
    pad_start = pad_end - padded
    cnt_ref[...] = counts
    pstart_ref[...] = pad_start
    nused_ref[...] = lax.shift_right_logical(jnp.max(pad_end, axis=0, keepdims=True), shift)
    blk_row = lax.broadcasted_iota(I32, (N_EXPERTS, n_blk_pad), 1) * ROW_BLOCK
    blk_ref[...] = jnp.minimum(jnp.sum((pad_end <= blk_row).astype(I32), axis=0, keepdims=True), N_EXPERTS - 1)

    def place_chunk(c, carry):
        off = pl.multiple_of(c * ch, ch)
        for k in range(TOP_K):
            ek = e_ref[k:k + 1, pl.ds(off, ch)]
            base = jnp.sum(jnp.where(eiota == ek, pad_start, 0), axis=0, keepdims=True)
            dest_ref[k:k + 1, pl.ds(off, ch)] = dest_ref[k:k + 1, pl.ds(off, ch)] + base
        return carry

    lax.fori_loop(0, n_chunks, place_chunk, 0)


def _route(lgt, n_blk):
    n_tok = lgt.shape[1]
    assert n_tok % ROUTE_CHUNK == 0
    n_blk_pad = -(-n_blk // LANES) * LANES
    full = lambda shape: pl.BlockSpec(shape, lambda i: (0,) * len(shape))
    return pl.pallas_call(
        functools.partial(_route_body, n_tok=n_tok, n_blk_pad=n_blk_pad),
        grid=(1,),
        in_specs=[full((N_EXPERTS, n_tok))],
        out_specs=[full((TOP_K, n_tok)), full((TOP_K, n_tok)), full((TOP_K, n_tok)), full((1, n_blk_pad)),
                   full((N_EXPERTS, 1)), full((N_EXPERTS, 1)), full((1, 1))],
        out_shape=[jax.ShapeDtypeStruct((TOP_K, n_tok), I32),
                   jax.ShapeDtypeStruct((TOP_K, n_tok), F32),
                   jax.ShapeDtypeStruct((TOP_K, n_tok), I32),
                   jax.ShapeDtypeStruct((1, n_blk_pad), I32),
                   jax.ShapeDtypeStruct((N_EXPERTS, 1), I32),
                   jax.ShapeDtypeStruct((N_EXPERTS, 1), I32),
                   jax.ShapeDtypeStruct((1, 1), I32)],
        compiler_params=pltpu.CompilerParams(dimension_semantics=("arbitrary",), vmem_limit_bytes=VMEM_LIMIT_BYTES),
        name="route",
    )(lgt)


def _dispatch_body(cnt_ref, pstart_ref, nused_ref, dest_ref, xp_ref, xs_ref, out_ref, zbuf, sem, *,
                   n_prompt_tiles, n_blk):
    i = pl.program_id(0)

    @pl.when(i == 0)
    def _():
        zbuf[...] = jnp.zeros_like(zbuf)
        for e in range(N_EXPERTS):
            cnt = cnt_ref[e]
            first = pstart_ref[e] + cnt
            n_pad = (-cnt) & (ROW_BLOCK - 1)

            def fill(j, c):
                _tile_copy(zbuf, 0, out_ref, first + j, sem).start()
                return c

            def drain(j, c):
                _tile_copy(zbuf, 0, out_ref, first + j, sem).wait()
                return c

            lax.fori_loop(0, n_pad, fill, 0)
            lax.fori_loop(0, n_pad, drain, 0)

        def fill_blk(b, c):
            blk_rows = ROW_BLOCK * ROW_TILE
            cp = pltpu.make_async_copy(zbuf, out_ref.at[pl.ds(pl.multiple_of(b * blk_rows, blk_rows), blk_rows), :], sem)
            cp.start()
            cp.wait()
            return c

        lax.fori_loop(nused_ref[0], n_blk, fill_blk, 0)

    def scatter_rows(src):
        def issue(g, c):
            for u in range(ISSUE_UNROLL):
                r = g * ISSUE_UNROLL + u
                for k in range(TOP_K):
                    _tile_copy(src, r, out_ref, dest_ref[0, 0, k * TOK_TILE + r], sem).start(priority=k % 2)
            return c

        lax.fori_loop(0, TOK_TILE // ISSUE_UNROLL, issue, 0)
        for k in range(TOP_K):
            pltpu.make_async_copy(src, out_ref.at[pl.ds(0, TOK_TILE * ROW_TILE), :], sem).wait()

    @pl.when(i < n_prompt_tiles)
    def _():
        scatter_rows(xp_ref)

    @pl.when(i >= n_prompt_tiles)
    def _():
        scatter_rows(xs_ref)


def _dispatch(cnt, pstart, nused, dest_tiles, xn2_p, xn2_s, n_blk):
    tile_rows = TOK_TILE * ROW_TILE
    n_p = xn2_p.shape[0] // tile_rows
    n_s = xn2_s.shape[0] // tile_rows
    assert xn2_p.shape[0] % tile_rows == 0 and xn2_s.shape[0] % tile_rows == 0
    return pl.pallas_call(
        functools.partial(_dispatch_body, n_prompt_tiles=n_p, n_blk=n_blk),
        grid_spec=pltpu.PrefetchScalarGridSpec(
            num_scalar_prefetch=3,
            grid=(n_p + n_s,),
            in_specs=[
                pl.BlockSpec((1, 1, TOP_K * TOK_TILE), lambda i, *_: (i, 0, 0), memory_space=pltpu.SMEM),
                pl.BlockSpec((tile_rows, LANES), lambda i, *_: (jnp.minimum(i, n_p - 1), 0)),
                pl.BlockSpec((tile_rows, LANES), lambda i, *_: (jnp.maximum(i - n_p, 0), 0)),
            ],
            out_specs=pl.BlockSpec(memory_space=pl.ANY),
            scratch_shapes=[pltpu.VMEM((ROW_BLOCK * ROW_TILE, LANES), F32), pltpu.SemaphoreType.DMA],
        ),
        out_shape=jax.ShapeDtypeStruct((n_blk * ROW_BLOCK * ROW_TILE, LANES), F32),
        compiler_params=pltpu.CompilerParams(dimension_semantics=("arbitrary",), vmem_limit_bytes=VMEM_LIMIT_BYTES),
        name="dispatch",
    )(cnt, pstart, nused, dest_tiles, xn2_p, xn2_s)


def _ffn_body(blk_ref, nused_ref, x_ref, wgu_ref, bgu_ref, wd_ref, bd_ref, o_ref):
    i = pl.program_id(0)

    @pl.when(i < nused_ref[0])
    def _():
        x = _load_row_tiles(x_ref, ROW_BLOCK).astype(BF16)
        gu = jnp.dot(x, wgu_ref[0], preferred_element_type=F32) + bgu_ref[0]
        gate = jnp.minimum(gu[:, 0:D_EXPERT], SWIGLU_LIMIT)
        up = jnp.clip(gu[:, D_EXPERT:2 * D_EXPERT], -SWIGLU_LIMIT, SWIGLU_LIMIT)
        act = (up + 1.0) * (gate * jax.nn.sigmoid(SWIGLU_ALPHA * gate))
        out = jnp.dot(act.astype(BF16), wd_ref[0], preferred_element_type=F32) + bd_ref[0]
        _store_row_tiles(o_ref, out, ROW_BLOCK)

    @pl.when(i >= nused_ref[0])
    def _():
        o_ref[...] = jnp.zeros_like(o_ref)


def _ffn(blk_exp, nused, xs, wgu, bgu, wd, bd):
    blk_rows = ROW_BLOCK * ROW_TILE
    n_blk = xs.shape[0] // blk_rows
    row_map = lambda i, blk, nu: (jnp.minimum(i, nu[0] - 1), 0)
    exp_map = lambda i, blk, nu: (blk[jnp.minimum(i, nu[0] - 1)], 0, 0)
    return pl.pallas_call(
        _ffn_body,
        grid_spec=pltpu.PrefetchScalarGridSpec(
            num_scalar_prefetch=2,
            grid=(n_blk,),
            in_specs=[
                pl.BlockSpec((blk_rows, LANES), row_map),
                pl.BlockSpec((1, D_MODEL, 2 * D_EXPERT), exp_map),
                pl.BlockSpec((1, 1, 2 * D_EXPERT), exp_map),
                pl.BlockSpec((1, D_EXPERT, D_MODEL), exp_map),
                pl.BlockSpec((1, 1, D_MODEL), exp_map),
            ],
            out_specs=pl.BlockSpec((blk_rows, LANES), lambda i, blk, nu: (i, 0)),
        ),
        out_shape=jax.ShapeDtypeStruct(xs.shape, F32),
        compiler_params=pltpu.CompilerParams(dimension_semantics=("arbitrary",), vmem_limit_bytes=VMEM_LIMIT_BYTES),
        name="ffn",
    )(blk_exp, nused, xs, wgu, bgu, wd, bd)


def _combine_body(dest_ref, ys_ref, x1_ref, gate_ref, gf_ref, y_ref, buf, sem):
    def issue(g, c):
        for u in range(ISSUE_UNROLL):
            r = g * ISSUE_UNROLL + u
            for k in range(TOP_K):
                _tile_copy(ys_ref, dest_ref[0, 0, k * TOK_TILE + r], buf.at[k], r, sem).start(priority=k % 2)
        return c

    lax.fori_loop(0, TOK_TILE // ISSUE_UNROLL, issue, 0)
    for k in range(TOP_K):
        pltpu.make_async_copy(ys_ref.at[pl.ds(0, TOK_TILE * ROW_TILE), :], buf.at[k], sem).wait()
    gates = gate_ref[...]
    moe = _load_row_tiles(buf.at[0], TOK_TILE) * gates[:, 0:1]
    for k in range(1, TOP_K):
        moe = moe + _load_row_tiles(buf.at[k], TOK_TILE) * gates[:, k:k + 1]
    x2 = x1_ref[...] + moe
    ms = jnp.mean(x2 * x2, axis=-1, keepdims=True)
    y_ref[...] = x2 * lax.rsqrt(ms + RMS_EPS) * gf_ref[...]


def _combine(dest_tiles, ys, x1, gates_t, gf, tile0):
    n = x1.shape[0]
    assert n % TOK_TILE == 0
    return pl.pallas_call(
        _combine_body,
        grid=(n // TOK_TILE,),
        in_specs=[
            pl.BlockSpec((1, 1, TOP_K * TOK_TILE), lambda i: (tile0 + i, 0, 0), memory_space=pltpu.SMEM),
            pl.BlockSpec(memory_space=pl.ANY),
            pl.BlockSpec((TOK_TILE, D_MODEL), lambda i: (i, 0)),
            pl.BlockSpec((TOK_TILE, TOP_K), lambda i: (tile0 + i, 0)),
            pl.BlockSpec((1, D_MODEL), lambda i: (0, 0)),
        ],
        out_specs=pl.BlockSpec((TOK_TILE, D_MODEL), lambda i: (i, 0)),
        out_shape=jax.ShapeDtypeStruct((n, D_MODEL), F32),
        scratch_shapes=[pltpu.VMEM((TOP_K, TOK_TILE * ROW_TILE, LANES), F32), pltpu.SemaphoreType.DMA],
        compiler_params=pltpu.CompilerParams(dimension_semantics=("arbitrary",), vmem_limit_bytes=VMEM_LIMIT_BYTES),
        name="combine",
    )(dest_tiles, ys, x1, gates_t, gf)


def _pack_gate_weights(w_rg, w_ig):
    def bd(w):
        w = w.reshape(D_RNN // MXU_DIM, GATE_PACK, LRU_BLOCK, LRU_BLOCK)
        eye = jnp.eye(GATE_PACK, dtype=w.dtype)
        return jnp.einsum('jpik,pq->jpiqk', w, eye).reshape(D_RNN // MXU_DIM, MXU_DIM, MXU_DIM)
    return jnp.concatenate([bd(w_rg), bd(w_ig)], axis=-1).astype(BF16)


def kernel(x_prompt, x_sample, state_conv, state_h, norm1_g, w_in, conv_w, conv_b, w_rg, b_rg, w_ig, b_ig, lru_lambda, ln_v_g, ln_v_b, w_s, b_s, w_proj_a, w_proj_b, w_out, norm2_g, w_router, b_router, w_gate_up, b_gate_up, w_down, b_down, final_norm_g):
    assert norm1_g.shape[0] == 1, "single layer"
    B, S, _ = x_prompt.shape
    Bs, Ss, _ = x_sample.shape
    row = lambda p: p.reshape(1, -1).astype(F32)
    wts = (
        row(norm1_g[0]), w_in[0].astype(BF16), conv_w[0], row(conv_b[0]), _pack_gate_weights(w_rg[0], w_ig[0]),
        row(b_rg[0]), row(b_ig[0]), row(lru_lambda[0]), row(ln_v_g[0]), row(ln_v_b[0]), w_s[0], b_s[0].T,
        w_proj_a[0].astype(BF16), w_proj_b[0].astype(BF16), w_out[0].astype(BF16), row(norm2_g[0]),
        w_router[0].T.astype(BF16), b_router[0].reshape(N_EXPERTS, 1),
    )
    x1_p, xn2_p, lgt_p, convl_p, hl_p = _mixer(
        x_prompt, jnp.zeros((B, SUBLANES, D_RNN), F32), jnp.zeros((B, 1, D_RNN), F32), wts,
        nseq=1, seg=min(MIX_ROWS, S), start_zero=True, want_v=False)
    conv0_s = jnp.pad(state_conv[0], ((0, 0), (SUBLANES - (CONV_WIDTH - 1), 0), (0, 0)))
    x1_s, xn2_s, lgt_s, convl_s, hl_s, v_s = _mixer(
        x_sample, conv0_s, state_h[0][:, None, :], wts,
        nseq=Bs, seg=Ss, start_zero=False, want_v=True)

    n_p, n_s = B * S, Bs * Ss
    n_tok = n_p + n_s
    n_blk = -(-(n_tok * TOP_K + N_EXPERTS * (ROW_BLOCK - 1)) // ROW_BLOCK)
    lgt = jnp.concatenate([lgt_p, lgt_s], axis=1)
    _, gate, dest, blk_exp, cnt, pstart, nused = _route(lgt, n_blk)
    n_tiles = n_tok // TOK_TILE
    dest_tiles = dest.reshape(TOP_K, n_tiles, TOK_TILE).transpose(1, 0, 2).reshape(n_tiles, 1, TOP_K * TOK_TILE)
    gates_t = gate.T
    cnt, pstart, nused, blk_exp = cnt.reshape(-1), pstart.reshape(-1), nused.reshape(-1), blk_exp.reshape(-1)

    xs = _dispatch(cnt, pstart, nused, dest_tiles, xn2_p, xn2_s, n_blk)
    ys = _ffn(blk_exp, nused, xs, w_gate_up[0].astype(BF16), b_gate_up[0][:, None, :], w_down[0].astype(BF16),
              b_down[0][:, None, :])
    gf = row(final_norm_g)
    y_p = _combine(dest_tiles, ys, x1_p.reshape(n_p, D_MODEL), gates_t, gf, 0).reshape(B, S, D_MODEL)
    y_s = _combine(dest_tiles, ys, x1_s.reshape(n_s, D_MODEL), gates_t, gf, n_p // TOK_TILE).reshape(Bs, Ss, D_MODEL)

    keep = slice(SUBLANES - (CONV_WIDTH - 1), SUBLANES)
    return (y_p, y_s, convl_p[None, :, keep, :], hl_p.reshape(1, B, D_RNN), convl_s[None, :, keep, :],
            hl_s.reshape(1, Bs, D_RNN), v_s[None])
```

```python
import functools

import jax
import jax.numpy as jnp
from jax import lax
from jax.experimental import pallas as pl
from jax.experimental.pallas import tpu as pltpu

F32 = jnp.float32
BF16 = jnp.bfloat16
I32 = jnp.int32

D_MODEL = 1024
D_RNN = D_MODEL
N_LRU_BLOCKS = 16
LRU_BLOCK = D_RNN // N_LRU_BLOCKS
CONV_WIDTH = 4
LRU_C = 8.0
MLP_CHUNK = 128
N_MLP_GROUPS = 4
MLP_GROUP_DIM = D_MODEL // 8
D_MLP = N_MLP_GROUPS * MLP_GROUP_DIM
N_IN = 2 * D_RNN + 2 * D_MLP + 2 * D_MODEL
N_EXPERTS = 32
TOP_K = 4
D_EXPERT = D_MODEL
SWIGLU_LIMIT = 7.0
SWIGLU_ALPHA = 1.702
RMS_EPS = 1e-6
LN_EPS = 1e-5
PAST_LEN = 1024

SUBLANES = 8
LANES = 128
MXU_DIM = 256
ROW_TILE = D_MODEL // LANES
VMEM_LIMIT_BYTES = 56 * 1024 * 1024

MIX_ROWS = 256
PROJ_CUTS = (512, 3584, 4608)
ROUTE_CHUNK = 256
ROW_BLOCK = 512
TOK_TILE = 256
ISSUE_UNROLL = 8
COMBINE_GROUP = 32
GATE_PACK = MXU_DIM // LRU_BLOCK


def _const_spec(shape):
    nd = len(shape)
    return pl.BlockSpec(shape, lambda *_: (0,) * nd, pipeline_mode=pl.Buffered(1))


def _store_row_tiles(ref, val, n):
    for j in range(ROW_TILE):
        ref[pl.ds(j, n, stride=ROW_TILE), :] = val[:, j * LANES:(j + 1) * LANES]


def _load_row_tiles(ref, n):
    return jnp.concatenate([ref[pl.ds(j, n, stride=ROW_TILE), :] for j in range(ROW_TILE)], axis=1)


def _row_tile(ref, row):
    start = row * ROW_TILE
    if not isinstance(start, int):
        start = pl.multiple_of(start, ROW_TILE)
    return ref.at[pl.ds(start, ROW_TILE), :]


def _tile_copy(src, src_row, dst, dst_row, sem):
    return pltpu.make_async_copy(_row_tile(src, src_row), _row_tile(dst, dst_row), sem)


def _norm1(x, g1_ref):
    ms = jnp.mean(x * x, axis=-1, keepdims=True)
    return (x * lax.rsqrt(ms + RMS_EPS) * g1_ref[...]).astype(BF16)


def _project(x, g1_ref, win_ref):
    return jnp.dot(_norm1(x, g1_ref), win_ref[...], preferred_element_type=F32)


def _restart_state(conv0_ref, h0_ref, state):
    xa_buf, h_tiles, _, _ = state
    nseq = xa_buf.shape[0]
    xa_buf[:, 0:SUBLANES, :] = conv0_ref[...]
    _store_row_tiles(h_tiles, h0_ref[...].reshape(nseq, D_RNN), nseq)


def _mix_tile(x, z, s, wrefs, orefs, state, *, nseq, seg, mc, start_zero, want_v, fill_mxu=lambda slot: None):
    (convw_ref, convb_ref, wg_ref, brg_ref, big_ref, lam_ref, lng_ref, lnb_ref, ws_ref, bst_ref, wpa_ref, wpb_ref,
     wout_ref, g2_ref, wrt_ref, brt_ref) = wrefs
    if want_v:
        x1_ref, xn2_ref, lgt_ref, convl_ref, hl_ref, v_ref = orefs
    else:
        x1_ref, xn2_ref, lgt_ref, convl_ref, hl_ref = orefs
        v_ref = None
    xa_buf, h_tiles, scan_a, scan_b = state
    rows = nseq * seg
    xa = z[:, 0:D_RNN]
    ga = z[:, D_RNN:2 * D_RNN]
    zu = z[:, 2 * D_RNN:2 * D_RNN + D_MLP]
    zv = z[:, 2 * D_RNN + D_MLP:2 * D_RNN + 2 * D_MLP]
    gate_a = z[:, 2 * D_RNN + 2 * D_MLP:2 * D_RNN + 2 * D_MLP + D_MODEL]
    gate_b = z[:, 2 * D_RNN + 2 * D_MLP + D_MODEL:N_IN]

    cw = convw_ref[...]
    xc_parts = []
    for q in range(nseq):
        xa_buf[q, SUBLANES:SUBLANES + seg, :] = xa[q * seg:(q + 1) * seg]
        acc = convb_ref[...] + xa[q * seg:(q + 1) * seg] * cw[CONV_WIDTH - 1:CONV_WIDTH]
        for k in range(CONV_WIDTH - 1):
            off = SUBLANES - (CONV_WIDTH - 1) + k
            acc = acc + xa_buf[q, off:off + seg, :] * cw[k:k + 1]
        xc_parts.append(acc)
        xa_buf[q, 0:SUBLANES, :] = xa_buf[q, seg:seg + SUBLANES, :]
    xc = xc_parts[0] if nseq == 1 else jnp.concatenate(xc_parts, axis=0)
    convl_ref[...] = xa_buf[:, 0:SUBLANES, :]

    u = jax.nn.gelu(zu)
    gv = jax.nn.gelu(zv)
    mu = jnp.mean(gv, axis=-1, keepdims=True)
    vc = gv - mu
    var = jnp.mean(vc * vc, axis=-1, keepdims=True)
    v = vc * lax.rsqrt(var + LN_EPS) * lng_ref[...] + lnb_ref[...]
    if want_v:
        v_ref[...] = v.reshape(nseq, seg, D_MLP)
    vb = v.astype(BF16)

    xcb = xc.astype(BF16)
    r_parts, i_parts = [], []
    for j in range(D_RNN // MXU_DIM):
        gj = jnp.dot(xcb[:, j * MXU_DIM:(j + 1) * MXU_DIM], wg_ref[j], preferred_element_type=F32)
        r_parts.append(gj[:, 0:MXU_DIM])
        i_parts.append(gj[:, MXU_DIM:2 * MXU_DIM])
    fill_mxu(0)

    tri = lax.broadcasted_iota(I32, (mc, mc), 0) >= lax.broadcasted_iota(I32, (mc, mc), 1)
    ws_tri = [jnp.where(tri, ws_ref[g][0:mc, 0:mc], 0.0).astype(BF16) for g in range(N_MLP_GROUPS)]
    bst = bst_ref[...]
    mix_rows = []
    for c in range(rows // mc):
        cols = []
        for g in range(N_MLP_GROUPS):
            vg = vb[c * mc:(c + 1) * mc, g * MLP_GROUP_DIM:(g + 1) * MLP_GROUP_DIM]
            mg = jnp.dot(ws_tri[g], vg, preferred_element_type=F32) + bst[0:mc, g:g + 1]
            cols.append(mg)
        mix_rows.append(jnp.concatenate(cols, axis=1))
    vmix = mix_rows[0] if len(mix_rows) == 1 else jnp.concatenate(mix_rows, axis=0)

    r = jax.nn.sigmoid(jnp.concatenate(r_parts, axis=1) + brg_ref[...])
    ig = jax.nn.sigmoid(jnp.concatenate(i_parts, axis=1) + big_ref[...])
    lam = lam_ref[...]
    softplus_neg = jnp.maximum(-lam, 0.0) + jnp.log1p(jnp.exp(-jnp.abs(lam)))
    log_a = (-LRU_C) * r * softplus_neg
    a = jnp.exp(log_a)
    mult = jnp.sqrt(jnp.maximum(1.0 - a * a, 0.0))
    if start_zero:
        row = lax.broadcasted_iota(I32, (rows, 1), 0)
        mult = jnp.where(jnp.logical_and(row == 0, s == 0), 1.0, mult)
    bterm = mult * ig * xc

    _store_row_tiles(scan_a, a, rows)
    _store_row_tiles(scan_b, bterm, rows)
    for q in range(nseq):
        hq = h_tiles[q * ROW_TILE:(q + 1) * ROW_TILE, :]
        for t in range(seg):
            r = (q * seg + t) * ROW_TILE
            hq = scan_a[r:r + ROW_TILE, :] * hq + scan_b[r:r + ROW_TILE, :]
            scan_b[r:r + ROW_TILE, :] = hq
        h_tiles[q * ROW_TILE:(q + 1) * ROW_TILE, :] = hq
    h = _load_row_tiles(scan_b, rows)
    hl_ref[...] = _load_row_tiles(h_tiles, nseq).reshape(nseq, 1, D_RNN)

    ya = jnp.dot((h * jax.nn.gelu(ga)).astype(BF16), wpa_ref[...], preferred_element_type=F32)
    yb = jnp.dot((u * vmix).astype(BF16), wpb_ref[...], preferred_element_type=F32)
    fill_mxu(1)

    merged = jax.nn.sigmoid(gate_a) * ya + jax.nn.sigmoid(gate_b) * yb
    x1 = x + jnp.dot(merged.astype(BF16), wout_ref[...], preferred_element_type=F32)
    fill_mxu(2)
    x1_ref[...] = x1.reshape(nseq, seg, D_MODEL)

    ms2 = jnp.mean(x1 * x1, axis=-1, keepdims=True)
    xn2 = x1 * lax.rsqrt(ms2 + RMS_EPS) * g2_ref[...]
    _store_row_tiles(xn2_ref, xn2, rows)
    lgt = lax.dot_general(wrt_ref[...], xn2.astype(BF16), (((1,), (1,)), ((), ())), preferred_element_type=F32)
    lgt_ref[...] = lgt + brt_ref[...]


N_MIX_WEIGHTS = 18


def _mixer_whole_body(x_ref, conv0_ref, h0_ref, g1_ref, win_ref, *rest, n_out, **static):
    wrefs = rest[:N_MIX_WEIGHTS - 2]
    orefs = rest[N_MIX_WEIGHTS - 2:N_MIX_WEIGHTS - 2 + n_out]
    state = rest[N_MIX_WEIGHTS - 2 + n_out:]
    x = x_ref[...].reshape(static["nseq"] * static["seg"], D_MODEL)
    _restart_state(conv0_ref, h0_ref, state)
    _mix_tile(x, _project(x, g1_ref, win_ref), 0, wrefs, orefs, state, **static)


def _mixer_pipe_body(xnext_ref, x_ref, conv0_ref, h0_ref, g1_ref, win_ref, *rest, n_out, n_s, **static):
    wrefs = rest[:N_MIX_WEIGHTS - 2]
    orefs = rest[N_MIX_WEIGHTS - 2:N_MIX_WEIGHTS - 2 + n_out]
    *state, z_even, z_odd = rest[N_MIX_WEIGHTS - 2 + n_out:]
    t = pl.program_id(0)
    rows = static["seg"]
    s_prev = lax.rem(jnp.maximum(t - 1, 0), n_s)

    @pl.when(t == 0)
    def _():
        z_odd[...] = jnp.zeros_like(z_odd)

    @pl.when(s_prev == 0)
    def _():
        _restart_state(conv0_ref, h0_ref, state)

    def step(z_new, z_done):
        xn = _norm1(xnext_ref[...].reshape(rows, D_MODEL), g1_ref)

        def project(lo, hi):
            z_new[:, lo:hi] = jnp.dot(xn, win_ref[:, lo:hi], preferred_element_type=F32)

        cuts = (0,) + PROJ_CUTS + (N_IN,)
        project(cuts[0], cuts[1])
        _mix_tile(x_ref[...].reshape(rows, D_MODEL), z_done, s_prev, wrefs, orefs, state,
                  fill_mxu=lambda slot: project(cuts[slot + 1], cuts[slot + 2]), **static)

    @pl.when(lax.rem(t, 2) == 0)
    def _():
        step(z_even, z_odd)

    @pl.when(lax.rem(t, 2) == 1)
    def _():
        step(z_odd, z_even)


def _mixer(x, conv0, h0, wts, *, nseq, seg, start_zero, want_v):
    B, S, _ = x.shape
    assert len(wts) == N_MIX_WEIGHTS
    assert B % nseq == 0 and S % seg == 0 and seg % SUBLANES == 0
    assert nseq == 1 or (seg == S and nseq == B)
    mc = min(seg, MLP_CHUNK)
    assert seg % mc == 0
    rows = nseq * seg
    n_s = S // seg
    n_tiles = (B // nseq) * n_s
    pipelined = nseq == 1
    done = (lambda t: jnp.maximum(t - 1, 0)) if pipelined else (lambda t: t)
    seq_spec = lambda w, tile: pl.BlockSpec((nseq, seg, w), lambda t: (tile(t) // n_s, tile(t) % n_s, 0))
    state_spec = lambda r: pl.BlockSpec((nseq, r, D_RNN), lambda t: (done(t) // n_s, 0, 0))
    in_specs = [seq_spec(D_MODEL, done), state_spec(SUBLANES), state_spec(1)] + [_const_spec(w.shape) for w in wts]
    out_shape = [
        jax.ShapeDtypeStruct((B, S, D_MODEL), F32),
        jax.ShapeDtypeStruct((B * S * ROW_TILE, LANES), F32),
        jax.ShapeDtypeStruct((N_EXPERTS, B * S), F32),
        jax.ShapeDtypeStruct((B, SUBLANES, D_RNN), F32),
        jax.ShapeDtypeStruct((B, 1, D_RNN), F32),
    ]
    out_specs = [
        seq_spec(D_MODEL, done), pl.BlockSpec((rows * ROW_TILE, LANES), lambda t: (done(t), 0)),
        pl.BlockSpec((N_EXPERTS, rows), lambda t: (0, done(t))),
        state_spec(SUBLANES), state_spec(1),
    ]
    if want_v:
        out_shape.append(jax.ShapeDtypeStruct((B, S, D_MLP), F32))
        out_specs.append(seq_spec(D_MLP, done))
    static = dict(n_out=len(out_shape), nseq=nseq, seg=seg, mc=mc, start_zero=start_zero, want_v=want_v)
    scratch = [pltpu.VMEM((nseq, seg + SUBLANES, D_RNN), F32), pltpu.VMEM((nseq * ROW_TILE, LANES), F32),
               pltpu.VMEM((rows * ROW_TILE, LANES), F32), pltpu.VMEM((rows * ROW_TILE, LANES), F32)]
    if pipelined:
        body = functools.partial(_mixer_pipe_body, n_s=n_s, **static)
        grid = (n_tiles + 1,)
        in_specs = [seq_spec(D_MODEL, lambda t: jnp.minimum(t, n_tiles - 1))] + in_specs
        scratch += [pltpu.VMEM((rows, N_IN), F32), pltpu.VMEM((rows, N_IN), F32)]
        operands = (x, x, conv0, h0, *wts)
    else:
        body = functools.partial(_mixer_whole_body, **static)
        grid = (n_tiles,)
        operands = (x, conv0, h0, *wts)
    return pl.pallas_call(
        body,
        grid=grid,
        in_specs=in_specs,
        out_specs=out_specs,
        out_shape=out_shape,
        scratch_shapes=scratch,
        compiler_params=pltpu.CompilerParams(dimension_semantics=("arbitrary",), vmem_limit_bytes=VMEM_LIMIT_BYTES),
        name="mixer",
    )(*operands)


def _route_body(lgt_ref, e_ref, gate_ref, dest_ref, blk_ref, cnt_ref, pstart_ref, nused_ref, *, n_tok, n_blk_pad):
    ch = ROUTE_CHUNK
    n_chunks = n_tok // ch
    eiota = lax.broadcasted_iota(I32, (N_EXPERTS, ch), 0)
    upper = (lax.broadcasted_iota(I32, (ch, ch), 0) < lax.broadcasted_iota(I32, (ch, ch), 1)).astype(BF16)

    def rank_chunk(c, counts):
        off = pl.multiple_of(c * ch, ch)
        work = lgt_ref[:, pl.ds(off, ch)]
        sel = jnp.zeros((N_EXPERTS, ch), F32)
        es, vs = [], []
        for _ in range(TOP_K):
            m = jnp.max(work, axis=0, keepdims=True)
            idx = jnp.min(jnp.where(work == m, eiota, N_EXPERTS), axis=0, keepdims=True)
            hit = eiota == idx
            es.append(idx)
            vs.append(m)
            work = jnp.where(hit, -jnp.inf, work)
            sel = jnp.where(hit, 1.0, sel)
        exps = [jnp.exp(v - vs[0]) for v in vs]
        inv = 1.0 / (exps[0] + exps[1] + exps[2] + exps[3])
        before = jnp.dot(sel.astype(BF16), upper, preferred_element_type=F32) + counts
        for k in range(TOP_K):
            rank = jnp.sum(jnp.where(eiota == es[k], before, 0.0), axis=0, keepdims=True)
            e_ref[k:k + 1, pl.ds(off, ch)] = es[k]
            gate_ref[k:k + 1, pl.ds(off, ch)] = exps[k] * inv
            dest_ref[k:k + 1, pl.ds(off, ch)] = rank.astype(I32)
        return counts + jnp.sum(sel, axis=1, keepdims=True)

    counts = lax.fori_loop(0, n_chunks, rank_chunk, jnp.zeros((N_EXPERTS, 1), F32)).astype(I32)
    shift = ROW_BLOCK.bit_length() - 1
    padded = lax.shift_left(lax.shift_right_logical(counts + (ROW_BLOCK - 1), shift), shift)
    col = lax.broadcasted_iota(I32, (N_EXPERTS, 1), 0)
    pad_end = jnp.zeros((N_EXPERTS, 1), I32)
    for e in range(N_EXPERTS):
        tot = jnp.sum(jnp.where(col <= e, padded, 0), axis=0, keepdims=True)
        pad_end = jnp.where(col == e, tot, pad_end)
    pad_start = pad_end - padded
    cnt_ref[...] = counts
    pstart_ref[...] = pad_start
    nused_ref[...] = lax.shift_right_logical(jnp.max(pad_end, axis=0, keepdims=True), shift)
    blk_row = lax.broadcasted_iota(I32, (N_EXPERTS, n_blk_pad), 1) * ROW_BLOCK
    blk_ref[...] = jnp.minimum(jnp.sum((pad_end <= blk_row).astype(I32), axis=0, keepdims=True), N_EXPERTS - 1)

    def place_chunk(c, carry):
        off = pl.multiple_of(c * ch, ch)
        for k in range(TOP_K):
            ek = e_ref[k:k + 1, pl.ds(off, ch)]
            base = jnp.sum(jnp.where(eiota == ek, pad_start, 0), axis=0, keepdims=True)
            dest_ref[k:k + 1, pl.ds(off, ch)] = dest_ref[k:k + 1, pl.ds(off, ch)] + base
        return carry

    lax.fori_loop(0, n_chunks, place_chunk, 0)


def _route(lgt, n_blk):
    n_tok = lgt.shape[1]
    assert n_tok % ROUTE_CHUNK == 0
    n_blk_pad = -(-n_blk // LANES) * LANES
    full = lambda shape: pl.BlockSpec(shape, lambda i: (0,) * len(shape))
    return pl.pallas_call(
        functools.partial(_route_body, n_tok=n_tok, n_blk_pad=n_blk_pad),
        grid=(1,),
        in_specs=[full((N_EXPERTS, n_tok))],
        out_specs=[full((TOP_K, n_tok)), full((TOP_K, n_tok)), full((TOP_K, n_tok)), full((1, n_blk_pad)),
                   full((N_EXPERTS, 1)), full((N_EXPERTS, 1)), full((1, 1))],
        out_shape=[jax.ShapeDtypeStruct((TOP_K, n_tok), I32),
                   jax.ShapeDtypeStruct((TOP_K, n_tok), F32),
                   jax.ShapeDtypeStruct((TOP_K, n_tok), I32),
                   jax.ShapeDtypeStruct((1, n_blk_pad), I32),
                   jax.ShapeDtypeStruct((N_EXPERTS, 1), I32),
                   jax.ShapeDtypeStruct((N_EXPERTS, 1), I32),
                   jax.ShapeDtypeStruct((1, 1), I32)],
        compiler_params=pltpu.CompilerParams(dimension_semantics=("arbitrary",), vmem_limit_bytes=VMEM_LIMIT_BYTES),
        name="route",
    )(lgt)


def _dispatch_body(cnt_ref, pstart_ref, nused_ref, dest_ref, xp_ref, xs_ref, out_ref, zbuf, sem, *,
                   n_prompt_tiles, n_blk):
    i = pl.program_id(0)

    @pl.when(i == 0)
    def _():
        zbuf[...] = jnp.zeros_like(zbuf)
        sizes = [1 << b for b in reversed(range(ROW_BLOCK.bit_length() - 1))]

        def pad_copies(e):
            cnt = cnt_ref[e]
            first = pstart_ref[e] + cnt
            n_pad = (-cnt) & (ROW_BLOCK - 1)
            for size in sizes:
                start = first + (n_pad & ~(2 * size - 1))
                dst = out_ref.at[pl.ds(pl.multiple_of(start * ROW_TILE, ROW_TILE), size * ROW_TILE), :]
                yield (n_pad & size) != 0, pltpu.make_async_copy(zbuf.at[0:size * ROW_TILE, :], dst, sem)

        for e in range(N_EXPERTS):
            for needed, cp in pad_copies(e):
                pl.when(needed)(cp.start)
        for e in range(N_EXPERTS):
            for needed, cp in pad_copies(e):
                pl.when(needed)(cp.wait)

        def fill_blk(b, c):
            blk_rows = ROW_BLOCK * ROW_TILE
            cp = pltpu.make_async_copy(zbuf, out_ref.at[pl.ds(pl.multiple_of(b * blk_rows, blk_rows), blk_rows), :], sem)
            cp.start()
            cp.wait()
            return c

        lax.fori_loop(nused_ref[0], n_blk, fill_blk, 0)

    def scatter_rows(src):
        def issue(g, c):
            for u in range(ISSUE_UNROLL):
                r = g * ISSUE_UNROLL + u
                for k in range(TOP_K):
                    _tile_copy(src, r, out_ref, dest_ref[0, 0, k * TOK_TILE + r], sem).start(priority=k % 2)
            return c

        lax.fori_loop(0, TOK_TILE // ISSUE_UNROLL, issue, 0)
        for k in range(TOP_K):
            pltpu.make_async_copy(src, out_ref.at[pl.ds(0, TOK_TILE * ROW_TILE), :], sem).wait()

    @pl.when(i < n_prompt_tiles)
    def _():
        scatter_rows(xp_ref)

    @pl.when(i >= n_prompt_tiles)
    def _():
        scatter_rows(xs_ref)


def _dispatch(cnt, pstart, nused, dest_tiles, xn2_p, xn2_s, n_blk):
    tile_rows = TOK_TILE * ROW_TILE
    n_p = xn2_p.shape[0] // tile_rows
    n_s = xn2_s.shape[0] // tile_rows
    assert xn2_p.shape[0] % tile_rows == 0 and xn2_s.shape[0] % tile_rows == 0
    return pl.pallas_call(
        functools.partial(_dispatch_body, n_prompt_tiles=n_p, n_blk=n_blk),
        grid_spec=pltpu.PrefetchScalarGridSpec(
            num_scalar_prefetch=3,
            grid=(n_p + n_s,),
            in_specs=[
                pl.BlockSpec((1, 1, TOP_K * TOK_TILE), lambda i, *_: (i, 0, 0), memory_space=pltpu.SMEM),
                pl.BlockSpec((tile_rows, LANES), lambda i, *_: (jnp.minimum(i, n_p - 1), 0)),
                pl.BlockSpec((tile_rows, LANES), lambda i, *_: (jnp.maximum(i - n_p, 0), 0)),
            ],
            out_specs=pl.BlockSpec(memory_space=pl.ANY),
            scratch_shapes=[pltpu.VMEM((ROW_BLOCK * ROW_TILE, LANES), F32), pltpu.SemaphoreType.DMA],
        ),
        out_shape=jax.ShapeDtypeStruct((n_blk * ROW_BLOCK * ROW_TILE, LANES), F32),
        compiler_params=pltpu.CompilerParams(dimension_semantics=("arbitrary",), vmem_limit_bytes=VMEM_LIMIT_BYTES),
        name="dispatch",
    )(cnt, pstart, nused, dest_tiles, xn2_p, xn2_s)


def _ffn_body(blk_ref, nused_ref, x_ref, wgu_ref, bgu_ref, wd_ref, bd_ref, o_ref, wgu_bf, wd_bf):
    i = pl.program_id(0)
    last = nused_ref[0] - 1
    expert = blk_ref[jnp.minimum(i, last)]
    prev_expert = blk_ref[jnp.minimum(jnp.maximum(i - 1, 0), last)]

    @pl.when(jnp.logical_or(i == 0, expert != prev_expert))
    def _():
        wgu_bf[...] = wgu_ref[0].astype(BF16)
        wd_bf[...] = wd_ref[0].astype(BF16)

    @pl.when(i < nused_ref[0])
    def _():
        x = _load_row_tiles(x_ref, ROW_BLOCK).astype(BF16)
        gu = jnp.dot(x, wgu_bf[...], preferred_element_type=F32) + bgu_ref[0]
        gate = jnp.minimum(gu[:, 0:D_EXPERT], SWIGLU_LIMIT)
        up = jnp.clip(gu[:, D_EXPERT:2 * D_EXPERT], -SWIGLU_LIMIT, SWIGLU_LIMIT)
        act = (up + 1.0) * (gate * jax.nn.sigmoid(SWIGLU_ALPHA * gate))
        out = jnp.dot(act.astype(BF16), wd_bf[...], preferred_element_type=F32) + bd_ref[0]
        _store_row_tiles(o_ref, out, ROW_BLOCK)

    @pl.when(i >= nused_ref[0])
    def _():
        o_ref[...] = jnp.zeros_like(o_ref)


def _ffn(blk_exp, nused, xs, wgu, bgu, wd, bd):
    blk_rows = ROW_BLOCK * ROW_TILE
    n_blk = xs.shape[0] // blk_rows
    row_map = lambda i, blk, nu: (jnp.minimum(i, nu[0] - 1), 0)
    exp_map = lambda i, blk, nu: (blk[jnp.minimum(i, nu[0] - 1)], 0, 0)
    return pl.pallas_call(
        _ffn_body,
        grid_spec=pltpu.PrefetchScalarGridSpec(
            num_scalar_prefetch=2,
            grid=(n_blk,),
            in_specs=[
                pl.BlockSpec((blk_rows, LANES), row_map),
                pl.BlockSpec((1, D_MODEL, 2 * D_EXPERT), exp_map),
                pl.BlockSpec((1, 1, 2 * D_EXPERT), exp_map),
                pl.BlockSpec((1, D_EXPERT, D_MODEL), exp_map),
                pl.BlockSpec((1, 1, D_MODEL), exp_map),
            ],
            out_specs=pl.BlockSpec((blk_rows, LANES), lambda i, blk, nu: (i, 0)),
            scratch_shapes=[pltpu.VMEM((D_MODEL, 2 * D_EXPERT), BF16), pltpu.VMEM((D_EXPERT, D_MODEL), BF16)],
        ),
        out_shape=jax.ShapeDtypeStruct(xs.shape, F32),
        compiler_params=pltpu.CompilerParams(dimension_semantics=("arbitrary",), vmem_limit_bytes=VMEM_LIMIT_BYTES),
        name="ffn",
    )(blk_exp, nused, xs, wgu, bgu, wd, bd)


def _combine_body(dnext_ref, dcur_ref, ys_ref, x1_ref, gate_ref, gf_ref, y_ref, buf, sems, *, n_tiles):
    i = pl.program_id(0)
    cur = lax.rem(i, 2)
    n_groups = TOK_TILE // COMBINE_GROUP

    def issue_group(dref, slot, g):
        for u in range(COMBINE_GROUP):
            r = g * COMBINE_GROUP + u
            for k in range(TOP_K):
                _tile_copy(ys_ref, dref[0, 0, k * TOK_TILE + r], buf.at[slot, k], r, sems.at[slot]).start(priority=k % 2)

    def finish_group(g):
        base = pl.multiple_of(g * COMBINE_GROUP, COMBINE_GROUP)
        gates = gate_ref[pl.ds(base, COMBINE_GROUP), :]
        moe = None
        for k in range(TOP_K):
            rows = jnp.concatenate(
                [buf[cur, k, pl.ds(base * ROW_TILE + j, COMBINE_GROUP, stride=ROW_TILE), :] for j in range(ROW_TILE)],
                axis=1)
            term = rows * gates[:, k:k + 1]
            moe = term if moe is None else moe + term
        x2 = x1_ref[pl.ds(base, COMBINE_GROUP), :] + moe
        ms = jnp.mean(x2 * x2, axis=-1, keepdims=True)
        y_ref[pl.ds(base, COMBINE_GROUP), :] = x2 * lax.rsqrt(ms + RMS_EPS) * gf_ref[...]

    @pl.when(i == 0)
    def _():
        def first(g, c):
            issue_group(dcur_ref, 0, g)
            return c

        lax.fori_loop(0, n_groups, first, 0)

    for k in range(TOP_K):
        pltpu.make_async_copy(ys_ref.at[pl.ds(0, TOK_TILE * ROW_TILE), :], buf.at[cur, k], sems.at[cur]).wait()

    @pl.when(i + 1 < n_tiles)
    def _():
        def both(g, c):
            issue_group(dnext_ref, 1 - cur, g)
            finish_group(g)
            return c

        lax.fori_loop(0, n_groups, both, 0)

    @pl.when(i + 1 >= n_tiles)
    def _():
        def last(g, c):
            finish_group(g)
            return c

        lax.fori_loop(0, n_groups, last, 0)


def _combine(dest_tiles, ys, x1, gates_t, gf, tile0):
    n = x1.shape[0]
    assert n % TOK_TILE == 0
    n_tiles = n // TOK_TILE
    dest_spec = lambda ahead: pl.BlockSpec(
        (1, 1, TOP_K * TOK_TILE), lambda i: (tile0 + jnp.minimum(i + ahead, n_tiles - 1), 0, 0), memory_space=pltpu.SMEM)
    return pl.pallas_call(
        functools.partial(_combine_body, n_tiles=n_tiles),
        grid=(n_tiles,),
        in_specs=[
            dest_spec(1), dest_spec(0),
            pl.BlockSpec(memory_space=pl.ANY),
            pl.BlockSpec((TOK_TILE, D_MODEL), lambda i: (i, 0)),
            pl.BlockSpec((TOK_TILE, TOP_K), lambda i: (tile0 + i, 0)),
            pl.BlockSpec((1, D_MODEL), lambda i: (0, 0)),
        ],
        out_specs=pl.BlockSpec((TOK_TILE, D_MODEL), lambda i: (i, 0)),
        out_shape=jax.ShapeDtypeStruct((n, D_MODEL), F32),
        scratch_shapes=[pltpu.VMEM((2, TOP_K, TOK_TILE * ROW_TILE, LANES), F32), pltpu.SemaphoreType.DMA((2,))],
        compiler_params=pltpu.CompilerParams(dimension_semantics=("arbitrary",), vmem_limit_bytes=VMEM_LIMIT_BYTES),
        name="combine",
    )(dest_tiles, dest_tiles, ys, x1, gates_t, gf)


def _pack_gate_weights(w_rg, w_ig):
    def bd(w):
        w = w.reshape(D_RNN // MXU_DIM, GATE_PACK, LRU_BLOCK, LRU_BLOCK)
        eye = jnp.eye(GATE_PACK, dtype=w.dtype)
        return jnp.einsum('jpik,pq->jpiqk', w, eye).reshape(D_RNN // MXU_DIM, MXU_DIM, MXU_DIM)
    return jnp.concatenate([bd(w_rg), bd(w_ig)], axis=-1).astype(BF16)


def kernel(x_prompt, x_sample, state_conv, state_h, norm1_g, w_in, conv_w, conv_b, w_rg, b_rg, w_ig, b_ig, lru_lambda, ln_v_g, ln_v_b, w_s, b_s, w_proj_a, w_proj_b, w_out, norm2_g, w_router, b_router, w_gate_up, b_gate_up, w_down, b_down, final_norm_g):
    assert norm1_g.shape[0] == 1, "single layer"
    B, S, _ = x_prompt.shape
    Bs, Ss, _ = x_sample.shape
    row = lambda p: p.reshape(1, -1).astype(F32)
    wts = (
        row(norm1_g[0]), w_in[0].astype(BF16), conv_w[0], row(conv_b[0]), _pack_gate_weights(w_rg[0], w_ig[0]),
        row(b_rg[0]), row(b_ig[0]), row(lru_lambda[0]), row(ln_v_g[0]), row(ln_v_b[0]), w_s[0], b_s[0].T,
        w_proj_a[0].astype(BF16), w_proj_b[0].astype(BF16), w_out[0].astype(BF16), row(norm2_g[0]),
        w_router[0].T.astype(BF16), b_router[0].reshape(N_EXPERTS, 1),
    )
    x1_p, xn2_p, lgt_p, convl_p, hl_p = _mixer(
        x_prompt, jnp.zeros((B, SUBLANES, D_RNN), F32), jnp.zeros((B, 1, D_RNN), F32), wts,
        nseq=1, seg=min(MIX_ROWS, S), start_zero=True, want_v=False)
    conv0_s = jnp.pad(state_conv[0], ((0, 0), (SUBLANES - (CONV_WIDTH - 1), 0), (0, 0)))
    x1_s, xn2_s, lgt_s, convl_s, hl_s, v_s = _mixer(
        x_sample, conv0_s, state_h[0][:, None, :], wts,
        nseq=Bs, seg=Ss, start_zero=False, want_v=True)

    n_p, n_s = B * S, Bs * Ss
    n_tok = n_p + n_s
    n_blk = -(-(n_tok * TOP_K + N_EXPERTS * (ROW_BLOCK - 1)) // ROW_BLOCK)
    lgt = jnp.concatenate([lgt_p, lgt_s], axis=1)
    _, gate, dest, blk_exp, cnt, pstart, nused = _route(lgt, n_blk)
    assert n_tok % TOK_TILE == 0
    n_tiles = n_tok // TOK_TILE
    dest_tiles = dest.reshape(TOP_K, n_tiles, TOK_TILE).transpose(1, 0, 2).reshape(n_tiles, 1, TOP_K * TOK_TILE)
    gates_t = gate.T
    cnt, pstart, nused, blk_exp = cnt.reshape(-1), pstart.reshape(-1), nused.reshape(-1), blk_exp.reshape(-1)

    xs = _dispatch(cnt, pstart, nused, dest_tiles, xn2_p, xn2_s, n_blk)
    ys = _ffn(blk_exp, nused, xs, w_gate_up[0], b_gate_up[0][:, None, :], w_down[0], b_down[0][:, None, :])
    gf = row(final_norm_g)
    y_p = _combine(dest_tiles, ys, x1_p.reshape(n_p, D_MODEL), gates_t, gf, 0).reshape(B, S, D_MODEL)
    y_s = _combine(dest_tiles, ys, x1_s.reshape(n_s, D_MODEL), gates_t, gf, n_p // TOK_TILE).reshape(Bs, Ss, D_MODEL)

    keep = slice(SUBLANES - (CONV_WIDTH - 1), SUBLANES)
    return (y_p, y_s, convl_p[None, :, keep, :], hl_p.reshape(1, B, D_RNN), convl_s[None, :, keep, :],
            hl_s.reshape(1, Bs, D_RNN), v_s[None])
```

```python
import functools

import jax
import jax.numpy as jnp
from jax import lax
from jax.experimental import pallas as pl
from jax.experimental.pallas import tpu as pltpu

F32 = jnp.float32
BF16 = jnp.bfloat16
I32 = jnp.int32

D_MODEL = 1024
D_RNN = D_MODEL
N_LRU_BLOCKS = 16
LRU_BLOCK = D_RNN // N_LRU_BLOCKS
CONV_WIDTH = 4
LRU_C = 8.0
MLP_CHUNK = 128
N_MLP_GROUPS = 4
MLP_GROUP_DIM = D_MODEL // 8
D_MLP = N_MLP_GROUPS * MLP_GROUP_DIM
N_IN = 2 * D_RNN + 2 * D_MLP + 2 * D_MODEL
N_EXPERTS = 32
TOP_K = 4
D_EXPERT = D_MODEL
SWIGLU_LIMIT = 7.0
SWIGLU_ALPHA = 1.702
RMS_EPS = 1e-6
LN_EPS = 1e-5
PAST_LEN = 1024

SUBLANES = 8
LANES = 128
MXU_DIM = 256
ROW_TILE = D_MODEL // LANES
VMEM_LIMIT_BYTES = 56 * 1024 * 1024

MIX_ROWS = 256
PROJ_CUTS = (512, 3584, 4608)
ROUTE_CHUNK = 256
ROW_BLOCK = 512
TOK_TILE = 256
ISSUE_UNROLL = 8
COMBINE_GROUP = 32
GATE_PACK = MXU_DIM // LRU_BLOCK


def _const_spec(shape):
    nd = len(shape)
    return pl.BlockSpec(shape, lambda *_: (0,) * nd, pipeline_mode=pl.Buffered(1))


def _store_row_tiles(ref, val, n):
    for j in range(ROW_TILE):
        ref[pl.ds(j, n, stride=ROW_TILE), :] = val[:, j * LANES:(j + 1) * LANES]


def _load_row_tiles(ref, n):
    return jnp.concatenate([ref[pl.ds(j, n, stride=ROW_TILE), :] for j in range(ROW_TILE)], axis=1)


def _row_tile(ref, row):
    start = row * ROW_TILE
    if not isinstance(start, int):
        start = pl.multiple_of(start, ROW_TILE)
    return ref.at[pl.ds(start, ROW_TILE), :]


def _tile_copy(src, src_row, dst, dst_row, sem):
    return pltpu.make_async_copy(_row_tile(src, src_row), _row_tile(dst, dst_row), sem)


def _norm1(x, g1_ref):
    ms = jnp.mean(x * x, axis=-1, keepdims=True)
    return (x * lax.rsqrt(ms + RMS_EPS) * g1_ref[...]).astype(BF16)


def _project(x, g1_ref, win_ref):
    return jnp.dot(_norm1(x, g1_ref), win_ref[...], preferred_element_type=F32)


def _restart_state(conv0_ref, h0_ref, state):
    xa_buf, h_tiles, _, _ = state
    nseq = xa_buf.shape[0]
    xa_buf[:, 0:SUBLANES, :] = conv0_ref[...]
    _store_row_tiles(h_tiles, h0_ref[...].reshape(nseq, D_RNN), nseq)


def _mix_tile(x, z, s, wrefs, orefs, state, *, nseq, seg, mc, start_zero, want_v, fill_mxu=lambda slot: None):
    (convw_ref, convb_ref, wg_ref, brg_ref, big_ref, lam_ref, lng_ref, lnb_ref, ws_ref, bst_ref, wpa_ref, wpb_ref,
     wout_ref, g2_ref, wrt_ref, brt_ref) = wrefs
    if want_v:
        x1_ref, xn2_ref, lgt_ref, convl_ref, hl_ref, v_ref = orefs
    else:
        x1_ref, xn2_ref, lgt_ref, convl_ref, hl_ref = orefs
        v_ref = None
    xa_buf, h_tiles, scan_a, scan_b = state
    rows = nseq * seg
    xa = z[:, 0:D_RNN]
    ga = z[:, D_RNN:2 * D_RNN]
    zu = z[:, 2 * D_RNN:2 * D_RNN + D_MLP]
    zv = z[:, 2 * D_RNN + D_MLP:2 * D_RNN + 2 * D_MLP]
    gate_a = z[:, 2 * D_RNN + 2 * D_MLP:2 * D_RNN + 2 * D_MLP + D_MODEL]
    gate_b = z[:, 2 * D_RNN + 2 * D_MLP + D_MODEL:N_IN]

    cw = convw_ref[...]
    xc_parts = []
    for q in range(nseq):
        xa_buf[q, SUBLANES:SUBLANES + seg, :] = xa[q * seg:(q + 1) * seg]
        acc = convb_ref[...] + xa[q * seg:(q + 1) * seg] * cw[CONV_WIDTH - 1:CONV_WIDTH]
        for k in range(CONV_WIDTH - 1):
            off = SUBLANES - (CONV_WIDTH - 1) + k
            acc = acc + xa_buf[q, off:off + seg, :] * cw[k:k + 1]
        xc_parts.append(acc)
        xa_buf[q, 0:SUBLANES, :] = xa_buf[q, seg:seg + SUBLANES, :]
    xc = xc_parts[0] if nseq == 1 else jnp.concatenate(xc_parts, axis=0)
    convl_ref[...] = xa_buf[:, 0:SUBLANES, :]

    u = jax.nn.gelu(zu)
    gv = jax.nn.gelu(zv)
    mu = jnp.mean(gv, axis=-1, keepdims=True)
    vc = gv - mu
    var = jnp.mean(vc * vc, axis=-1, keepdims=True)
    v = vc * lax.rsqrt(var + LN_EPS) * lng_ref[...] + lnb_ref[...]
    if want_v:
        v_ref[...] = v.reshape(nseq, seg, D_MLP)
    vb = v.astype(BF16)

    xcb = xc.astype(BF16)
    r_parts, i_parts = [], []
    for j in range(D_RNN // MXU_DIM):
        gj = jnp.dot(xcb[:, j * MXU_DIM:(j + 1) * MXU_DIM], wg_ref[j], preferred_element_type=F32)
        r_parts.append(gj[:, 0:MXU_DIM])
        i_parts.append(gj[:, MXU_DIM:2 * MXU_DIM])
    fill_mxu(0)

    tri = lax.broadcasted_iota(I32, (mc, mc), 0) >= lax.broadcasted_iota(I32, (mc, mc), 1)
    ws_tri = [jnp.where(tri, ws_ref[g][0:mc, 0:mc], 0.0).astype(BF16) for g in range(N_MLP_GROUPS)]
    bst = bst_ref[...]
    mix_rows = []
    for c in range(rows // mc):
        cols = []
        for g in range(N_MLP_GROUPS):
            vg = vb[c * mc:(c + 1) * mc, g * MLP_GROUP_DIM:(g + 1) * MLP_GROUP_DIM]
            mg = jnp.dot(ws_tri[g], vg, preferred_element_type=F32) + bst[0:mc, g:g + 1]
            cols.append(mg)
        mix_rows.append(jnp.concatenate(cols, axis=1))
    vmix = mix_rows[0] if len(mix_rows) == 1 else jnp.concatenate(mix_rows, axis=0)

    r = jax.nn.sigmoid(jnp.concatenate(r_parts, axis=1) + brg_ref[...])
    ig = jax.nn.sigmoid(jnp.concatenate(i_parts, axis=1) + big_ref[...])
    lam = lam_ref[...]
    softplus_neg = jnp.maximum(-lam, 0.0) + jnp.log1p(jnp.exp(-jnp.abs(lam)))
    log_a = (-LRU_C) * r * softplus_neg
    a = jnp.exp(log_a)
    mult = jnp.sqrt(jnp.maximum(1.0 - a * a, 0.0))
    if start_zero:
        row = lax.broadcasted_iota(I32, (rows, 1), 0)
        mult = jnp.where(jnp.logical_and(row == 0, s == 0), 1.0, mult)
    bterm = mult * ig * xc

    _store_row_tiles(scan_a, a, rows)
    _store_row_tiles(scan_b, bterm, rows)
    for q in range(nseq):
        hq = h_tiles[q * ROW_TILE:(q + 1) * ROW_TILE, :]
        for t in range(seg):
            r = (q * seg + t) * ROW_TILE
            hq = scan_a[r:r + ROW_TILE, :] * hq + scan_b[r:r + ROW_TILE, :]
            scan_b[r:r + ROW_TILE, :] = hq
        h_tiles[q * ROW_TILE:(q + 1) * ROW_TILE, :] = hq
    h = _load_row_tiles(scan_b, rows)
    hl_ref[...] = _load_row_tiles(h_tiles, nseq).reshape(nseq, 1, D_RNN)

    ya = jnp.dot((h * jax.nn.gelu(ga)).astype(BF16), wpa_ref[...], preferred_element_type=F32)
    yb = jnp.dot((u * vmix).astype(BF16), wpb_ref[...], preferred_element_type=F32)
    fill_mxu(1)

    merged = jax.nn.sigmoid(gate_a) * ya + jax.nn.sigmoid(gate_b) * yb
    x1 = x + jnp.dot(merged.astype(BF16), wout_ref[...], preferred_element_type=F32)
    fill_mxu(2)
    x1_ref[...] = x1.reshape(nseq, seg, D_MODEL)

    ms2 = jnp.mean(x1 * x1, axis=-1, keepdims=True)
    xn2 = x1 * lax.rsqrt(ms2 + RMS_EPS) * g2_ref[...]
    _store_row_tiles(xn2_ref, xn2, rows)
    lgt = lax.dot_general(wrt_ref[...], xn2.astype(BF16), (((1,), (1,)), ((), ())), preferred_element_type=F32)
    lgt_ref[...] = lgt + brt_ref[...]


N_MIX_WEIGHTS = 18


def _mixer_whole_body(x_ref, conv0_ref, h0_ref, g1_ref, win_ref, *rest, n_out, **static):
    wrefs = rest[:N_MIX_WEIGHTS - 2]
    orefs = rest[N_MIX_WEIGHTS - 2:N_MIX_WEIGHTS - 2 + n_out]
    state = rest[N_MIX_WEIGHTS - 2 + n_out:]
    x = x_ref[...].reshape(static["nseq"] * static["seg"], D_MODEL)
    _restart_state(conv0_ref, h0_ref, state)
    _mix_tile(x, _project(x, g1_ref, win_ref), 0, wrefs, orefs, state, **static)


def _mixer_pipe_body(xnext_ref, x_ref, conv0_ref, h0_ref, g1_ref, win_ref, *rest, n_out, n_s, **static):
    wrefs = rest[:N_MIX_WEIGHTS - 2]
    orefs = rest[N_MIX_WEIGHTS - 2:N_MIX_WEIGHTS - 2 + n_out]
    *state, z_even, z_odd = rest[N_MIX_WEIGHTS - 2 + n_out:]
    t = pl.program_id(0)
    rows = static["seg"]
    s_prev = lax.rem(jnp.maximum(t - 1, 0), n_s)

    @pl.when(t == 0)
    def _():
        z_odd[...] = jnp.zeros_like(z_odd)

    @pl.when(s_prev == 0)
    def _():
        _restart_state(conv0_ref, h0_ref, state)

    def step(z_new, z_done):
        xn = _norm1(xnext_ref[...].reshape(rows, D_MODEL), g1_ref)

        def project(lo, hi):
            z_new[:, lo:hi] = jnp.dot(xn, win_ref[:, lo:hi], preferred_element_type=F32)

        cuts = (0,) + PROJ_CUTS + (N_IN,)
        project(cuts[0], cuts[1])
        _mix_tile(x_ref[...].reshape(rows, D_MODEL), z_done, s_prev, wrefs, orefs, state,
                  fill_mxu=lambda slot: project(cuts[slot + 1], cuts[slot + 2]), **static)

    @pl.when(lax.rem(t, 2) == 0)
    def _():
        step(z_even, z_odd)

    @pl.when(lax.rem(t, 2) == 1)
    def _():
        step(z_odd, z_even)


def _mixer(x, conv0, h0, wts, *, nseq, seg, start_zero, want_v):
    B, S, _ = x.shape
    assert len(wts) == N_MIX_WEIGHTS
    assert B % nseq == 0 and S % seg == 0 and seg % SUBLANES == 0
    assert nseq == 1 or (seg == S and nseq == B)
    mc = min(seg, MLP_CHUNK)
    assert seg % mc == 0
    rows = nseq * seg
    n_s = S // seg
    n_tiles = (B // nseq) * n_s
    pipelined = nseq == 1
    done = (lambda t: jnp.maximum(t - 1, 0)) if pipelined else (lambda t: t)
    seq_spec = lambda w, tile: pl.BlockSpec((nseq, seg, w), lambda t: (tile(t) // n_s, tile(t) % n_s, 0))
    state_spec = lambda r: pl.BlockSpec((nseq, r, D_RNN), lambda t: (done(t) // n_s, 0, 0))
    in_specs = [seq_spec(D_MODEL, done), state_spec(SUBLANES), state_spec(1)] + [_const_spec(w.shape) for w in wts]
    out_shape = [
        jax.ShapeDtypeStruct((B, S, D_MODEL), F32),
        jax.ShapeDtypeStruct((B * S * ROW_TILE, LANES), F32),
        jax.ShapeDtypeStruct((N_EXPERTS, B * S), F32),
        jax.ShapeDtypeStruct((B, SUBLANES, D_RNN), F32),
        jax.ShapeDtypeStruct((B, 1, D_RNN), F32),
    ]
    out_specs = [
        seq_spec(D_MODEL, done), pl.BlockSpec((rows * ROW_TILE, LANES), lambda t: (done(t), 0)),
        pl.BlockSpec((N_EXPERTS, rows), lambda t: (0, done(t))),
        state_spec(SUBLANES), state_spec(1),
    ]
    if want_v:
        out_shape.append(jax.ShapeDtypeStruct((B, S, D_MLP), F32))
        out_specs.append(seq_spec(D_MLP, done))
    static = dict(n_out=len(out_shape), nseq=nseq, seg=seg, mc=mc, start_zero=start_zero, want_v=want_v)
    scratch = [pltpu.VMEM((nseq, seg + SUBLANES, D_RNN), F32), pltpu.VMEM((nseq * ROW_TILE, LANES), F32),
               pltpu.VMEM((rows * ROW_TILE, LANES), F32), pltpu.VMEM((rows * ROW_TILE, LANES), F32)]
    if pipelined:
        body = functools.partial(_mixer_pipe_body, n_s=n_s, **static)
        grid = (n_tiles + 1,)
        in_specs = [seq_spec(D_MODEL, lambda t: jnp.minimum(t, n_tiles - 1))] + in_specs
        scratch += [pltpu.VMEM((rows, N_IN), F32), pltpu.VMEM((rows, N_IN), F32)]
        operands = (x, x, conv0, h0, *wts)
    else:
        body = functools.partial(_mixer_whole_body, **static)
        grid = (n_tiles,)
        operands = (x, conv0, h0, *wts)
    return pl.pallas_call(
        body,
        grid=grid,
        in_specs=in_specs,
        out_specs=out_specs,
        out_shape=out_shape,
        scratch_shapes=scratch,
        compiler_params=pltpu.CompilerParams(dimension_semantics=("arbitrary",), vmem_limit_bytes=VMEM_LIMIT_BYTES),
        name="mixer",
    )(*operands)


def _route_body(lgt_ref, e_ref, gate_ref, dest_ref, blk_ref, cnt_ref, pstart_ref, nused_ref, *, n_tok, n_blk_pad):
    ch = ROUTE_CHUNK
    n_chunks = n_tok // ch
    eiota = lax.broadcasted_iota(I32, (N_EXPERTS, ch), 0)
    upper = (lax.broadcasted_iota(I32, (ch, ch), 0) < lax.broadcasted_iota(I32, (ch, ch), 1)).astype(BF16)

    def rank_chunk(c, counts):
        off = pl.multiple_of(c * ch, ch)
        work = lgt_ref[:, pl.ds(off, ch)]
        sel = jnp.zeros((N_EXPERTS, ch), F32)
        es, vs = [], []
        for _ in range(TOP_K):
            m = jnp.max(work, axis=0, keepdims=True)
            idx = jnp.min(jnp.where(work == m, eiota, N_EXPERTS), axis=0, keepdims=True)
            hit = eiota == idx
            es.append(idx)
            vs.append(m)
            work = jnp.where(hit, -jnp.inf, work)
            sel = jnp.where(hit, 1.0, sel)
        exps = [jnp.exp(v - vs[0]) for v in vs]
        inv = 1.0 / (exps[0] + exps[1] + exps[2] + exps[3])
        before = jnp.dot(sel.astype(BF16), upper, preferred_element_type=F32) + counts
        for k in range(TOP_K):
            rank = jnp.sum(jnp.where(eiota == es[k], before, 0.0), axis=0, keepdims=True)
            e_ref[k:k + 1, pl.ds(off, ch)] = es[k]
            gate_ref[k:k + 1, pl.ds(off, ch)] = exps[k] * inv
            dest_ref[k:k + 1, pl.ds(off, ch)] = rank.astype(I32)
        return counts + jnp.sum(sel, axis=1, keepdims=True)

    counts = lax.fori_loop(0, n_chunks, rank_chunk, jnp.zeros((N_EXPERTS, 1), F32)).astype(I32)
    shift = ROW_BLOCK.bit_length() - 1
    padded = lax.shift_left(lax.shift_right_logical(counts + (ROW_BLOCK - 1), shift), shift)
    col = lax.broadcasted_iota(I32, (N_EXPERTS, 1), 0)
    pad_end = jnp.zeros((N_EXPERTS, 1), I32)
    for e in range(N_EXPERTS):
        tot = jnp.sum(jnp.where(col <= e, padded, 0), axis=0, keepdims=True)
        pad_end = jnp.where(col == e, tot, pad_end)
    pad_start = pad_end - padded
    cnt_ref[...] = counts
    pstart_ref[...] = pad_start
    nused_ref[...] = lax.shift_right_logical(jnp.max(pad_end, axis=0, keepdims=True), shift)
    blk_row = lax.broadcasted_iota(I32, (N_EXPERTS, n_blk_pad), 1) * ROW_BLOCK
    blk_ref[...] = jnp.minimum(jnp.sum((pad_end <= blk_row).astype(I32), axis=0, keepdims=True), N_EXPERTS - 1)

    def place_chunk(c, carry):
        off = pl.multiple_of(c * ch, ch)
        for k in range(TOP_K):
            ek = e_ref[k:k + 1, pl.ds(off, ch)]
            base = jnp.sum(jnp.where(eiota == ek, pad_start, 0), axis=0, keepdims=True)
            dest_ref[k:k + 1, pl.ds(off, ch)] = dest_ref[k:k + 1, pl.ds(off, ch)] + base
        return carry

    lax.fori_loop(0, n_chunks, place_chunk, 0)


def _route(lgt, n_blk):
    n_tok = lgt.shape[1]
    assert n_tok % ROUTE_CHUNK == 0
    n_blk_pad = -(-n_blk // LANES) * LANES
    full = lambda shape: pl.BlockSpec(shape, lambda i: (0,) * len(shape))
    return pl.pallas_call(
        functools.partial(_route_body, n_tok=n_tok, n_blk_pad=n_blk_pad),
        grid=(1,),
        in_specs=[full((N_EXPERTS, n_tok))],
        out_specs=[full((TOP_K, n_tok)), full((TOP_K, n_tok)), full((TOP_K, n_tok)), full((1, n_blk_pad)),
                   full((N_EXPERTS, 1)), full((N_EXPERTS, 1)), full((1, 1))],
        out_shape=[jax.ShapeDtypeStruct((TOP_K, n_tok), I32),
                   jax.ShapeDtypeStruct((TOP_K, n_tok), F32),
                   jax.ShapeDtypeStruct((TOP_K, n_tok), I32),
                   jax.ShapeDtypeStruct((1, n_blk_pad), I32),
                   jax.ShapeDtypeStruct((N_EXPERTS, 1), I32),
                   jax.ShapeDtypeStruct((N_EXPERTS, 1), I32),
                   jax.ShapeDtypeStruct((1, 1), I32)],
        compiler_params=pltpu.CompilerParams(dimension_semantics=("arbitrary",), vmem_limit_bytes=VMEM_LIMIT_BYTES),
        name="route",
    )(lgt)


def _dispatch_body(cnt_ref, pstart_ref, nused_ref, dest_ref, xp_ref, xs_ref, out_ref, zbuf, sem, *,
                   n_prompt_tiles, n_blk):
    i = pl.program_id(0)

    @pl.when(i == 0)
    def _():
        zbuf[...] = jnp.zeros_like(zbuf)
        sizes = [1 << b for b in reversed(range(ROW_BLOCK.bit_length() - 1))]

        def pad_copies(e):
            cnt = cnt_ref[e]
            first = pstart_ref[e] + cnt
            n_pad = (-cnt) & (ROW_BLOCK - 1)
            for size in sizes:
                start = first + (n_pad & ~(2 * size - 1))
                dst = out_ref.at[pl.ds(pl.multiple_of(start * ROW_TILE, ROW_TILE), size * ROW_TILE), :]
                yield (n_pad & size) != 0, pltpu.make_async_copy(zbuf.at[0:size * ROW_TILE, :], dst, sem)

        for e in range(N_EXPERTS):
            for needed, cp in pad_copies(e):
                pl.when(needed)(cp.start)
        for e in range(N_EXPERTS):
            for needed, cp in pad_copies(e):
                pl.when(needed)(cp.wait)

        def fill_blk(b, c):
            blk_rows = ROW_BLOCK * ROW_TILE
            cp = pltpu.make_async_copy(zbuf, out_ref.at[pl.ds(pl.multiple_of(b * blk_rows, blk_rows), blk_rows), :], sem)
            cp.start()
            cp.wait()
            return c

        lax.fori_loop(nused_ref[0], n_blk, fill_blk, 0)

    def scatter_rows(src):
        def issue(g, c):
            for u in range(ISSUE_UNROLL):
                r = g * ISSUE_UNROLL + u
                for k in range(TOP_K):
                    _tile_copy(src, r, out_ref, dest_ref[0, 0, k * TOK_TILE + r], sem).start(priority=k % 2)
            return c

        lax.fori_loop(0, TOK_TILE // ISSUE_UNROLL, issue, 0)
        for k in range(TOP_K):
            pltpu.make_async_copy(src, out_ref.at[pl.ds(0, TOK_TILE * ROW_TILE), :], sem).wait()

    @pl.when(i < n_prompt_tiles)
    def _():
        scatter_rows(xp_ref)

    @pl.when(i >= n_prompt_tiles)
    def _():
        scatter_rows(xs_ref)


def _dispatch(cnt, pstart, nused, dest_tiles, xn2_p, xn2_s, n_blk):
    tile_rows = TOK_TILE * ROW_TILE
    n_p = xn2_p.shape[0] // tile_rows
    n_s = xn2_s.shape[0] // tile_rows
    assert xn2_p.shape[0] % tile_rows == 0 and xn2_s.shape[0] % tile_rows == 0
    return pl.pallas_call(
        functools.partial(_dispatch_body, n_prompt_tiles=n_p, n_blk=n_blk),
        grid_spec=pltpu.PrefetchScalarGridSpec(
            num_scalar_prefetch=3,
            grid=(n_p + n_s,),
            in_specs=[
                pl.BlockSpec((1, 1, TOP_K * TOK_TILE), lambda i, *_: (i, 0, 0), memory_space=pltpu.SMEM),
                pl.BlockSpec((tile_rows, LANES), lambda i, *_: (jnp.minimum(i, n_p - 1), 0)),
                pl.BlockSpec((tile_rows, LANES), lambda i, *_: (jnp.maximum(i - n_p, 0), 0)),
            ],
            out_specs=pl.BlockSpec(memory_space=pl.ANY),
            scratch_shapes=[pltpu.VMEM((ROW_BLOCK * ROW_TILE, LANES), F32), pltpu.SemaphoreType.DMA],
        ),
        out_shape=jax.ShapeDtypeStruct((n_blk * ROW_BLOCK * ROW_TILE, LANES), F32),
        compiler_params=pltpu.CompilerParams(dimension_semantics=("arbitrary",), vmem_limit_bytes=VMEM_LIMIT_BYTES),
        name="dispatch",
    )(cnt, pstart, nused, dest_tiles, xn2_p, xn2_s)


def _ffn_body(blk_ref, nused_ref, runend_ref, x_ref, wgu_hbm, bgu_ref, wd_hbm, bd_ref, o_ref, wgu_f32, wd_f32,
              wgu_bf, wd_bf, sems, slot_ref):
    i = pl.program_id(0)
    last = nused_ref[0] - 1
    expert = blk_ref[jnp.minimum(i, last)]
    prev_expert = blk_ref[jnp.minimum(jnp.maximum(i - 1, 0), last)]

    def weight_copies(e, slot):
        return (pltpu.make_async_copy(wgu_hbm.at[e], wgu_f32.at[slot], sems.at[slot]),
                pltpu.make_async_copy(wd_hbm.at[e], wd_f32.at[slot], sems.at[slot]))

    @pl.when(i == 0)
    def _():
        slot_ref[0] = 0
        for cp in weight_copies(expert, 0):
            cp.start()

    @pl.when(jnp.logical_or(i == 0, expert != prev_expert))
    def _():
        slot = slot_ref[0]
        for cp in weight_copies(expert, slot):
            cp.wait()
        next_first = runend_ref[expert]

        @pl.when(next_first <= last)
        def _():
            for cp in weight_copies(blk_ref[jnp.minimum(next_first, last)], 1 - slot):
                cp.start()

        wgu_bf[...] = wgu_f32[slot].astype(BF16)
        wd_bf[...] = wd_f32[slot].astype(BF16)
        slot_ref[0] = 1 - slot

    @pl.when(i < nused_ref[0])
    def _():
        x = _load_row_tiles(x_ref, ROW_BLOCK).astype(BF16)
        gu = jnp.dot(x, wgu_bf[...], preferred_element_type=F32) + bgu_ref[0]
        gate = jnp.minimum(gu[:, 0:D_EXPERT], SWIGLU_LIMIT)
        up = jnp.clip(gu[:, D_EXPERT:2 * D_EXPERT], -SWIGLU_LIMIT, SWIGLU_LIMIT)
        act = (up + 1.0) * (gate * jax.nn.sigmoid(SWIGLU_ALPHA * gate))
        out = jnp.dot(act.astype(BF16), wd_bf[...], preferred_element_type=F32) + bd_ref[0]
        _store_row_tiles(o_ref, out, ROW_BLOCK)

    @pl.when(i >= nused_ref[0])
    def _():
        o_ref[...] = jnp.zeros_like(o_ref)


def _ffn(blk_exp, nused, run_end, xs, wgu, bgu, wd, bd):
    blk_rows = ROW_BLOCK * ROW_TILE
    n_blk = xs.shape[0] // blk_rows
    row_map = lambda i, blk, nu, re: (jnp.minimum(i, nu[0] - 1), 0)
    exp_map = lambda i, blk, nu, re: (blk[jnp.minimum(i, nu[0] - 1)], 0, 0)
    return pl.pallas_call(
        _ffn_body,
        grid_spec=pltpu.PrefetchScalarGridSpec(
            num_scalar_prefetch=3,
            grid=(n_blk,),
            in_specs=[
                pl.BlockSpec((blk_rows, LANES), row_map),
                pl.BlockSpec(memory_space=pl.ANY),
                pl.BlockSpec((1, 1, 2 * D_EXPERT), exp_map),
                pl.BlockSpec(memory_space=pl.ANY),
                pl.BlockSpec((1, 1, D_MODEL), exp_map),
            ],
            out_specs=pl.BlockSpec((blk_rows, LANES), lambda i, blk, nu, re: (i, 0)),
            scratch_shapes=[pltpu.VMEM((2, D_MODEL, 2 * D_EXPERT), F32), pltpu.VMEM((2, D_EXPERT, D_MODEL), F32),
                            pltpu.VMEM((D_MODEL, 2 * D_EXPERT), BF16), pltpu.VMEM((D_EXPERT, D_MODEL), BF16),
                            pltpu.SemaphoreType.DMA((2,)), pltpu.SMEM((1,), I32)],
        ),
        out_shape=jax.ShapeDtypeStruct(xs.shape, F32),
        compiler_params=pltpu.CompilerParams(dimension_semantics=("arbitrary",), vmem_limit_bytes=VMEM_LIMIT_BYTES),
        name="ffn",
    )(blk_exp, nused, run_end, xs, wgu, bgu, wd, bd)


def _combine_body(dnext_ref, dcur_ref, ys_ref, x1_ref, gate_ref, gf_ref, y_ref, buf, sems, *, n_tiles):
    i = pl.program_id(0)
    cur = lax.rem(i, 2)
    n_groups = TOK_TILE // COMBINE_GROUP

    def issue_group(dref, slot, g):
        for u in range(COMBINE_GROUP):
            r = g * COMBINE_GROUP + u
            for k in range(TOP_K):
                _tile_copy(ys_ref, dref[0, 0, k * TOK_TILE + r], buf.at[slot, k], r, sems.at[slot]).start(priority=k % 2)

    def finish_group(g):
        base = pl.multiple_of(g * COMBINE_GROUP, COMBINE_GROUP)
        gates = gate_ref[pl.ds(base, COMBINE_GROUP), :]
        moe = None
        for k in range(TOP_K):
            rows = jnp.concatenate(
                [buf[cur, k, pl.ds(base * ROW_TILE + j, COMBINE_GROUP, stride=ROW_TILE), :] for j in range(ROW_TILE)],
                axis=1)
            term = rows * gates[:, k:k + 1]
            moe = term if moe is None else moe + term
        x2 = x1_ref[pl.ds(base, COMBINE_GROUP), :] + moe
        ms = jnp.mean(x2 * x2, axis=-1, keepdims=True)
        y_ref[pl.ds(base, COMBINE_GROUP), :] = x2 * lax.rsqrt(ms + RMS_EPS) * gf_ref[...]

    @pl.when(i == 0)
    def _():
        def first(g, c):
            issue_group(dcur_ref, 0, g)
            return c

        lax.fori_loop(0, n_groups, first, 0)

    for k in range(TOP_K):
        pltpu.make_async_copy(ys_ref.at[pl.ds(0, TOK_TILE * ROW_TILE), :], buf.at[cur, k], sems.at[cur]).wait()

    @pl.when(i + 1 < n_tiles)
    def _():
        def both(g, c):
            issue_group(dnext_ref, 1 - cur, g)
            finish_group(g)
            return c

        lax.fori_loop(0, n_groups, both, 0)

    @pl.when(i + 1 >= n_tiles)
    def _():
        def last(g, c):
            finish_group(g)
            return c

        lax.fori_loop(0, n_groups, last, 0)


def _combine(dest_tiles, ys, x1, gates_t, gf, tile0):
    n = x1.shape[0]
    assert n % TOK_TILE == 0
    n_tiles = n // TOK_TILE
    dest_spec = lambda ahead: pl.BlockSpec(
        (1, 1, TOP_K * TOK_TILE), lambda i: (tile0 + jnp.minimum(i + ahead, n_tiles - 1), 0, 0), memory_space=pltpu.SMEM)
    return pl.pallas_call(
        functools.partial(_combine_body, n_tiles=n_tiles),
        grid=(n_tiles,),
        in_specs=[
            dest_spec(1), dest_spec(0),
            pl.BlockSpec(memory_space=pl.ANY),
            pl.BlockSpec((TOK_TILE, D_MODEL), lambda i: (i, 0)),
            pl.BlockSpec((TOK_TILE, TOP_K), lambda i: (tile0 + i, 0)),
            pl.BlockSpec((1, D_MODEL), lambda i: (0, 0)),
        ],
        out_specs=pl.BlockSpec((TOK_TILE, D_MODEL), lambda i: (i, 0)),
        out_shape=jax.ShapeDtypeStruct((n, D_MODEL), F32),
        scratch_shapes=[pltpu.VMEM((2, TOP_K, TOK_TILE * ROW_TILE, LANES), F32), pltpu.SemaphoreType.DMA((2,))],
        compiler_params=pltpu.CompilerParams(dimension_semantics=("arbitrary",), vmem_limit_bytes=VMEM_LIMIT_BYTES),
        name="combine",
    )(dest_tiles, dest_tiles, ys, x1, gates_t, gf)


def _pack_gate_weights(w_rg, w_ig):
    def bd(w):
        w = w.reshape(D_RNN // MXU_DIM, GATE_PACK, LRU_BLOCK, LRU_BLOCK)
        eye = jnp.eye(GATE_PACK, dtype=w.dtype)
        return jnp.einsum('jpik,pq->jpiqk', w, eye).reshape(D_RNN // MXU_DIM, MXU_DIM, MXU_DIM)
    return jnp.concatenate([bd(w_rg), bd(w_ig)], axis=-1).astype(BF16)


def kernel(x_prompt, x_sample, state_conv, state_h, norm1_g, w_in, conv_w, conv_b, w_rg, b_rg, w_ig, b_ig, lru_lambda, ln_v_g, ln_v_b, w_s, b_s, w_proj_a, w_proj_b, w_out, norm2_g, w_router, b_router, w_gate_up, b_gate_up, w_down, b_down, final_norm_g):
    assert norm1_g.shape[0] == 1, "single layer"
    B, S, _ = x_prompt.shape
    Bs, Ss, _ = x_sample.shape
    row = lambda p: p.reshape(1, -1).astype(F32)
    wts = (
        row(norm1_g[0]), w_in[0].astype(BF16), conv_w[0], row(conv_b[0]), _pack_gate_weights(w_rg[0], w_ig[0]),
        row(b_rg[0]), row(b_ig[0]), row(lru_lambda[0]), row(ln_v_g[0]), row(ln_v_b[0]), w_s[0], b_s[0].T,
        w_proj_a[0].astype(BF16), w_proj_b[0].astype(BF16), w_out[0].astype(BF16), row(norm2_g[0]),
        w_router[0].T.astype(BF16), b_router[0].reshape(N_EXPERTS, 1),
    )
    x1_p, xn2_p, lgt_p, convl_p, hl_p = _mixer(
        x_prompt, jnp.zeros((B, SUBLANES, D_RNN), F32), jnp.zeros((B, 1, D_RNN), F32), wts,
        nseq=1, seg=min(MIX_ROWS, S), start_zero=True, want_v=False)
    conv0_s = jnp.pad(state_conv[0], ((0, 0), (SUBLANES - (CONV_WIDTH - 1), 0), (0, 0)))
    x1_s, xn2_s, lgt_s, convl_s, hl_s, v_s = _mixer(
        x_sample, conv0_s, state_h[0][:, None, :], wts,
        nseq=Bs, seg=Ss, start_zero=False, want_v=True)

    n_p, n_s = B * S, Bs * Ss
    n_tok = n_p + n_s
    n_blk = -(-(n_tok * TOP_K + N_EXPERTS * (ROW_BLOCK - 1)) // ROW_BLOCK)
    lgt = jnp.concatenate([lgt_p, lgt_s], axis=1)
    _, gate, dest, blk_exp, cnt, pstart, nused = _route(lgt, n_blk)
    assert n_tok % TOK_TILE == 0
    n_tiles = n_tok // TOK_TILE
    dest_tiles = dest.reshape(TOP_K, n_tiles, TOK_TILE).transpose(1, 0, 2).reshape(n_tiles, 1, TOP_K * TOK_TILE)
    gates_t = gate.T
    cnt, pstart, nused, blk_exp = cnt.reshape(-1), pstart.reshape(-1), nused.reshape(-1), blk_exp.reshape(-1)

    xs = _dispatch(cnt, pstart, nused, dest_tiles, xn2_p, xn2_s, n_blk)
    run_end = (pstart + cnt + (ROW_BLOCK - 1)) // ROW_BLOCK
    ys = _ffn(blk_exp, nused, run_end, xs, w_gate_up[0], b_gate_up[0][:, None, :], w_down[0], b_down[0][:, None, :])
    gf = row(final_norm_g)
    y_p = _combine(dest_tiles, ys, x1_p.reshape(n_p, D_MODEL), gates_t, gf, 0).reshape(B, S, D_MODEL)
    y_s = _combine(dest_tiles, ys, x1_s.reshape(n_s, D_MODEL), gates_t, gf, n_p // TOK_TILE).reshape(Bs, Ss, D_MODEL)

    keep = slice(SUBLANES - (CONV_WIDTH - 1), SUBLANES)
    return (y_p, y_s, convl_p[None, :, keep, :], hl_p.reshape(1, B, D_RNN), convl_s[None, :, keep, :],
            hl_s.reshape(1, Bs, D_RNN), v_s[None])
```

```python
import functools

import jax
import jax.numpy as jnp
from jax import lax
from jax.experimental import pallas as pl
from jax.experimental.pallas import tpu as pltpu

F32 = jnp.float32
BF16 = jnp.bfloat16
I32 = jnp.int32

D_MODEL = 1024
D_RNN = D_MODEL
N_LRU_BLOCKS = 16
LRU_BLOCK = D_RNN // N_LRU_BLOCKS
CONV_WIDTH = 4
LRU_C = 8.0
MLP_CHUNK = 128
N_MLP_GROUPS = 4
MLP_GROUP_DIM = D_MODEL // 8
D_MLP = N_MLP_GROUPS * MLP_GROUP_DIM
N_IN = 2 * D_RNN + 2 * D_MLP + 2 * D_MODEL
N_EXPERTS = 32
TOP_K = 4
D_EXPERT = D_MODEL
SWIGLU_LIMIT = 7.0
SWIGLU_ALPHA = 1.702
RMS_EPS = 1e-6
LN_EPS = 1e-5

SUBLANES = 8
LANES = 128
MXU_DIM = 256
ROW_TILE = D_MODEL // LANES
VMEM_LIMIT_BYTES = 56 * 1024 * 1024

MIX_ROWS = 256
PROJ_CUTS = (512, 3584, 4608)
ROUTE_CHUNK = 256
ROW_BLOCK = 512
TOK_TILE = 256
ISSUE_UNROLL = 8
COMBINE_GROUP = 32
GATE_PACK = MXU_DIM // LRU_BLOCK


def _const_spec(shape):
    nd = len(shape)
    return pl.BlockSpec(shape, lambda *_: (0,) * nd, pipeline_mode=pl.Buffered(1))


def _store_row_tiles(ref, val, n):
    for j in range(ROW_TILE):
        ref[pl.ds(j, n, stride=ROW_TILE), :] = val[:, j * LANES:(j + 1) * LANES]


def _load_row_tiles(ref, n):
    return jnp.concatenate([ref[pl.ds(j, n, stride=ROW_TILE), :] for j in range(ROW_TILE)], axis=1)


def _row_tile(ref, row):
    start = row * ROW_TILE
    if not isinstance(start, int):
        start = pl.multiple_of(start, ROW_TILE)
    return ref.at[pl.ds(start, ROW_TILE), :]


def _tile_copy(src, src_row, dst, dst_row, sem):
    return pltpu.make_async_copy(_row_tile(src, src_row), _row_tile(dst, dst_row), sem)


def _norm1(x, g1_ref):
    ms = jnp.mean(x * x, axis=-1, keepdims=True)
    return (x * lax.rsqrt(ms + RMS_EPS) * g1_ref[...]).astype(BF16)


def _project(x, g1_ref, win_ref):
    return jnp.dot(_norm1(x, g1_ref), win_ref[...], preferred_element_type=F32)


def _restart_state(conv0_ref, h0_ref, state):
    xa_buf, h_tiles, _, _ = state
    nseq = xa_buf.shape[0]
    xa_buf[:, 0:SUBLANES, :] = conv0_ref[...]
    _store_row_tiles(h_tiles, h0_ref[...].reshape(nseq, D_RNN), nseq)


def _mix_tile(x, z, s, wrefs, orefs, state, *, nseq, seg, mc, start_zero, want_v, fill_mxu=lambda slot: None):
    (convw_ref, convb_ref, wg_ref, brg_ref, big_ref, lam_ref, lng_ref, lnb_ref, ws_ref, bst_ref, wpa_ref, wpb_ref,
     wout_ref, g2_ref, wrt_ref, brt_ref) = wrefs
    if want_v:
        x1_ref, xn2_ref, lgt_ref, convl_ref, hl_ref, v_ref = orefs
    else:
        x1_ref, xn2_ref, lgt_ref, convl_ref, hl_ref = orefs
        v_ref = None
    xa_buf, h_tiles, scan_a, scan_b = state
    rows = nseq * seg
    xa = z[:, 0:D_RNN]
    ga = z[:, D_RNN:2 * D_RNN]
    zu = z[:, 2 * D_RNN:2 * D_RNN + D_MLP]
    zv = z[:, 2 * D_RNN + D_MLP:2 * D_RNN + 2 * D_MLP]
    gate_a = z[:, 2 * D_RNN + 2 * D_MLP:2 * D_RNN + 2 * D_MLP + D_MODEL]
    gate_b = z[:, 2 * D_RNN + 2 * D_MLP + D_MODEL:N_IN]

    cw = convw_ref[...]
    xc_parts = []
    for q in range(nseq):
        xa_buf[q, SUBLANES:SUBLANES + seg, :] = xa[q * seg:(q + 1) * seg]
        acc = convb_ref[...] + xa[q * seg:(q + 1) * seg] * cw[CONV_WIDTH - 1:CONV_WIDTH]
        for k in range(CONV_WIDTH - 1):
            off = SUBLANES - (CONV_WIDTH - 1) + k
            acc = acc + xa_buf[q, off:off + seg, :] * cw[k:k + 1]
        xc_parts.append(acc)
        xa_buf[q, 0:SUBLANES, :] = xa_buf[q, seg:seg + SUBLANES, :]
    xc = xc_parts[0] if nseq == 1 else jnp.concatenate(xc_parts, axis=0)
    convl_ref[...] = xa_buf[:, 0:SUBLANES, :]

    u = jax.nn.gelu(zu)
    gv = jax.nn.gelu(zv)
    mu = jnp.mean(gv, axis=-1, keepdims=True)
    vc = gv - mu
    var = jnp.mean(vc * vc, axis=-1, keepdims=True)
    v = vc * lax.rsqrt(var + LN_EPS) * lng_ref[...] + lnb_ref[...]
    if want_v:
        v_ref[...] = v.reshape(nseq, seg, D_MLP)
    vb = v.astype(BF16)

    xcb = xc.astype(BF16)
    r_parts, i_parts = [], []
    for j in range(D_RNN // MXU_DIM):
        gj = jnp.dot(xcb[:, j * MXU_DIM:(j + 1) * MXU_DIM], wg_ref[j], preferred_element_type=F32)
        r_parts.append(gj[:, 0:MXU_DIM])
        i_parts.append(gj[:, MXU_DIM:2 * MXU_DIM])
    fill_mxu(0)

    tri = lax.broadcasted_iota(I32, (mc, mc), 0) >= lax.broadcasted_iota(I32, (mc, mc), 1)
    ws_tri = [jnp.where(tri, ws_ref[g][0:mc, 0:mc], 0.0).astype(BF16) for g in range(N_MLP_GROUPS)]
    bst = bst_ref[...]
    mix_rows = []
    for c in range(rows // mc):
        cols = []
        for g in range(N_MLP_GROUPS):
            vg = vb[c * mc:(c + 1) * mc, g * MLP_GROUP_DIM:(g + 1) * MLP_GROUP_DIM]
            mg = jnp.dot(ws_tri[g], vg, preferred_element_type=F32) + bst[0:mc, g:g + 1]
            cols.append(mg)
        mix_rows.append(jnp.concatenate(cols, axis=1))
    vmix = mix_rows[0] if len(mix_rows) == 1 else jnp.concatenate(mix_rows, axis=0)

    r = jax.nn.sigmoid(jnp.concatenate(r_parts, axis=1) + brg_ref[...])
    ig = jax.nn.sigmoid(jnp.concatenate(i_parts, axis=1) + big_ref[...])
    lam = lam_ref[...]
    softplus_neg = jnp.maximum(-lam, 0.0) + jnp.log1p(jnp.exp(-jnp.abs(lam)))
    log_a = (-LRU_C) * r * softplus_neg
    a = jnp.exp(log_a)
    mult = jnp.sqrt(jnp.maximum(1.0 - a * a, 0.0))
    if start_zero:
        row = lax.broadcasted_iota(I32, (rows, 1), 0)
        mult = jnp.where(jnp.logical_and(row == 0, s == 0), 1.0, mult)
    bterm = mult * ig * xc

    _store_row_tiles(scan_a, a, rows)
    _store_row_tiles(scan_b, bterm, rows)
    for q in range(nseq):
        hq = h_tiles[q * ROW_TILE:(q + 1) * ROW_TILE, :]
        for t in range(seg):
            r = (q * seg + t) * ROW_TILE
            hq = scan_a[r:r + ROW_TILE, :] * hq + scan_b[r:r + ROW_TILE, :]
            scan_b[r:r + ROW_TILE, :] = hq
        h_tiles[q * ROW_TILE:(q + 1) * ROW_TILE, :] = hq
    h = _load_row_tiles(scan_b, rows)
    hl_ref[...] = _load_row_tiles(h_tiles, nseq).reshape(nseq, 1, D_RNN)

    ya = jnp.dot((h * jax.nn.gelu(ga)).astype(BF16), wpa_ref[...], preferred_element_type=F32)
    yb = jnp.dot((u * vmix).astype(BF16), wpb_ref[...], preferred_element_type=F32)
    fill_mxu(1)

    merged = jax.nn.sigmoid(gate_a) * ya + jax.nn.sigmoid(gate_b) * yb
    x1 = x + jnp.dot(merged.astype(BF16), wout_ref[...], preferred_element_type=F32)
    fill_mxu(2)
    x1_ref[...] = x1.reshape(nseq, seg, D_MODEL)

    ms2 = jnp.mean(x1 * x1, axis=-1, keepdims=True)
    xn2 = x1 * lax.rsqrt(ms2 + RMS_EPS) * g2_ref[...]
    _store_row_tiles(xn2_ref, xn2, rows)
    lgt = lax.dot_general(wrt_ref[...], xn2.astype(BF16), (((1,), (1,)), ((), ())), preferred_element_type=F32)
    lgt_ref[...] = lgt + brt_ref[...]


N_MIX_WEIGHTS = 18


def _mixer_whole_body(x_ref, conv0_ref, h0_ref, g1_ref, win_ref, *rest, n_out, **static):
    wrefs = rest[:N_MIX_WEIGHTS - 2]
    orefs = rest[N_MIX_WEIGHTS - 2:N_MIX_WEIGHTS - 2 + n_out]
    state = rest[N_MIX_WEIGHTS - 2 + n_out:]
    x = x_ref[...].reshape(static["nseq"] * static["seg"], D_MODEL)
    _restart_state(conv0_ref, h0_ref, state)
    _mix_tile(x, _project(x, g1_ref, win_ref), 0, wrefs, orefs, state, **static)


def _mixer_pipe_body(xnext_ref, x_ref, conv0_ref, h0_ref, g1_ref, win_ref, *rest, n_out, n_s, **static):
    wrefs = rest[:N_MIX_WEIGHTS - 2]
    orefs = rest[N_MIX_WEIGHTS - 2:N_MIX_WEIGHTS - 2 + n_out]
    *state, z_even, z_odd = rest[N_MIX_WEIGHTS - 2 + n_out:]
    t = pl.program_id(0)
    rows = static["seg"]
    s_prev = lax.rem(jnp.maximum(t - 1, 0), n_s)

    @pl.when(t == 0)
    def _():
        z_odd[...] = jnp.zeros_like(z_odd)

    @pl.when(s_prev == 0)
    def _():
        _restart_state(conv0_ref, h0_ref, state)

    def step(z_new, z_done):
        xn = _norm1(xnext_ref[...].reshape(rows, D_MODEL), g1_ref)

        def project(lo, hi):
            z_new[:, lo:hi] = jnp.dot(xn, win_ref[:, lo:hi], preferred_element_type=F32)

        cuts = (0,) + PROJ_CUTS + (N_IN,)
        project(cuts[0], cuts[1])
        _mix_tile(x_ref[...].reshape(rows, D_MODEL), z_done, s_prev, wrefs, orefs, state,
                  fill_mxu=lambda slot: project(cuts[slot + 1], cuts[slot + 2]), **static)

    @pl.when(lax.rem(t, 2) == 0)
    def _():
        step(z_even, z_odd)

    @pl.when(lax.rem(t, 2) == 1)
    def _():
        step(z_odd, z_even)


def _mixer(x, conv0, h0, wts, *, nseq, seg, start_zero, want_v):
    B, S, _ = x.shape
    assert len(wts) == N_MIX_WEIGHTS
    assert B % nseq == 0 and S % seg == 0 and seg % SUBLANES == 0
    assert nseq == 1 or (seg == S and nseq == B)
    mc = min(seg, MLP_CHUNK)
    assert seg % mc == 0
    rows = nseq * seg
    n_s = S // seg
    n_tiles = (B // nseq) * n_s
    pipelined = nseq == 1
    done = (lambda t: jnp.maximum(t - 1, 0)) if pipelined else (lambda t: t)
    seq_spec = lambda w, tile: pl.BlockSpec((nseq, seg, w), lambda t: (tile(t) // n_s, tile(t) % n_s, 0))
    state_spec = lambda r: pl.BlockSpec((nseq, r, D_RNN), lambda t: (done(t) // n_s, 0, 0))
    in_specs = [seq_spec(D_MODEL, done), state_spec(SUBLANES), state_spec(1)] + [_const_spec(w.shape) for w in wts]
    out_shape = [
        jax.ShapeDtypeStruct((B, S, D_MODEL), F32),
        jax.ShapeDtypeStruct((B * S * ROW_TILE, LANES), F32),
        jax.ShapeDtypeStruct((N_EXPERTS, B * S), F32),
        jax.ShapeDtypeStruct((B, SUBLANES, D_RNN), F32),
        jax.ShapeDtypeStruct((B, 1, D_RNN), F32),
    ]
    out_specs = [
        seq_spec(D_MODEL, done), pl.BlockSpec((rows * ROW_TILE, LANES), lambda t: (done(t), 0)),
        pl.BlockSpec((N_EXPERTS, rows), lambda t: (0, done(t))),
        state_spec(SUBLANES), state_spec(1),
    ]
    if want_v:
        out_shape.append(jax.ShapeDtypeStruct((B, S, D_MLP), F32))
        out_specs.append(seq_spec(D_MLP, done))
    static = dict(n_out=len(out_shape), nseq=nseq, seg=seg, mc=mc, start_zero=start_zero, want_v=want_v)
    scratch = [pltpu.VMEM((nseq, seg + SUBLANES, D_RNN), F32), pltpu.VMEM((nseq * ROW_TILE, LANES), F32),
               pltpu.VMEM((rows * ROW_TILE, LANES), F32), pltpu.VMEM((rows * ROW_TILE, LANES), F32)]
    if pipelined:
        body = functools.partial(_mixer_pipe_body, n_s=n_s, **static)
        grid = (n_tiles + 1,)
        in_specs = [seq_spec(D_MODEL, lambda t: jnp.minimum(t, n_tiles - 1))] + in_specs
        scratch += [pltpu.VMEM((rows, N_IN), F32), pltpu.VMEM((rows, N_IN), F32)]
        operands = (x, x, conv0, h0, *wts)
    else:
        body = functools.partial(_mixer_whole_body, **static)
        grid = (n_tiles,)
        operands = (x, conv0, h0, *wts)
    return pl.pallas_call(
        body,
        grid=grid,
        in_specs=in_specs,
        out_specs=out_specs,
        out_shape=out_shape,
        scratch_shapes=scratch,
        compiler_params=pltpu.CompilerParams(dimension_semantics=("arbitrary",), vmem_limit_bytes=VMEM_LIMIT_BYTES),
        name="mixer",
    )(*operands)


def _route_body(*refs, n_toks, n_blk_pad):
    lgt_refs = refs[:len(n_toks)]
    e_ref, gate_ref, dest_ref, blk_ref, cnt_ref, pstart_ref, nused_ref = refs[len(n_toks):]
    ch = ROUTE_CHUNK
    n_chunks = sum(n_toks) // ch
    eiota = lax.broadcasted_iota(I32, (N_EXPERTS, ch), 0)
    upper = (lax.broadcasted_iota(I32, (ch, ch), 0) < lax.broadcasted_iota(I32, (ch, ch), 1)).astype(BF16)

    def rank_chunk(lgt_ref, first_chunk, c, counts):
        off = pl.multiple_of((first_chunk + c) * ch, ch)
        work = lgt_ref[:, pl.ds(pl.multiple_of(c * ch, ch), ch)]
        sel = jnp.zeros((N_EXPERTS, ch), F32)
        es, vs = [], []
        for _ in range(TOP_K):
            m = jnp.max(work, axis=0, keepdims=True)
            idx = jnp.min(jnp.where(work == m, eiota, N_EXPERTS), axis=0, keepdims=True)
            hit = eiota == idx
            es.append(idx)
            vs.append(m)
            work = jnp.where(hit, -jnp.inf, work)
            sel = jnp.where(hit, 1.0, sel)
        exps = [jnp.exp(v - vs[0]) for v in vs]
        inv = 1.0 / (exps[0] + exps[1] + exps[2] + exps[3])
        before = jnp.dot(sel.astype(BF16), upper, preferred_element_type=F32) + counts
        for k in range(TOP_K):
            rank = jnp.sum(jnp.where(eiota == es[k], before, 0.0), axis=0, keepdims=True)
            e_ref[k:k + 1, pl.ds(off, ch)] = es[k]
            gate_ref[k:k + 1, pl.ds(off, ch)] = exps[k] * inv
            dest_ref[k:k + 1, pl.ds(off, ch)] = rank.astype(I32)
        return counts + jnp.sum(sel, axis=1, keepdims=True)

    counts = jnp.zeros((N_EXPERTS, 1), F32)
    first_chunk = 0
    for lgt_ref, n in zip(lgt_refs, n_toks):
        counts = lax.fori_loop(0, n // ch, functools.partial(rank_chunk, lgt_ref, first_chunk), counts)
        first_chunk += n // ch
    counts = counts.astype(I32)
    shift = ROW_BLOCK.bit_length() - 1
    padded = lax.shift_left(lax.shift_right_logical(counts + (ROW_BLOCK - 1), shift), shift)
    col = lax.broadcasted_iota(I32, (N_EXPERTS, 1), 0)
    pad_end = jnp.zeros((N_EXPERTS, 1), I32)
    for e in range(N_EXPERTS):
        tot = jnp.sum(jnp.where(col <= e, padded, 0), axis=0, keepdims=True)
        pad_end = jnp.where(col == e, tot, pad_end)
    pad_start = pad_end - padded
    cnt_ref[...] = counts
    pstart_ref[...] = pad_start
    nused_ref[...] = lax.shift_right_logical(jnp.max(pad_end, axis=0, keepdims=True), shift)
    blk_row = lax.broadcasted_iota(I32, (N_EXPERTS, n_blk_pad), 1) * ROW_BLOCK
    blk_ref[...] = jnp.minimum(jnp.sum((pad_end <= blk_row).astype(I32), axis=0, keepdims=True), N_EXPERTS - 1)

    def place_chunk(c, carry):
        off = pl.multiple_of(c * ch, ch)
        for k in range(TOP_K):
            ek = e_ref[k:k + 1, pl.ds(off, ch)]
            base = jnp.sum(jnp.where(eiota == ek, pad_start, 0), axis=0, keepdims=True)
            dest_ref[k:k + 1, pl.ds(off, ch)] = dest_ref[k:k + 1, pl.ds(off, ch)] + base
        return carry

    lax.fori_loop(0, n_chunks, place_chunk, 0)


def _route(lgts, n_blk):
    n_toks = tuple(lgt.shape[1] for lgt in lgts)
    assert all(n % ROUTE_CHUNK == 0 for n in n_toks)
    n_tok = sum(n_toks)
    n_blk_pad = -(-n_blk // LANES) * LANES
    full = lambda shape: pl.BlockSpec(shape, lambda i: (0,) * len(shape))
    return pl.pallas_call(
        functools.partial(_route_body, n_toks=n_toks, n_blk_pad=n_blk_pad),
        grid=(1,),
        in_specs=[full(lgt.shape) for lgt in lgts],
        out_specs=[full((TOP_K, n_tok)), full((TOP_K, n_tok)), full((TOP_K, n_tok)), full((1, n_blk_pad)),
                   full((N_EXPERTS, 1)), full((N_EXPERTS, 1)), full((1, 1))],
        out_shape=[jax.ShapeDtypeStruct((TOP_K, n_tok), I32),
                   jax.ShapeDtypeStruct((TOP_K, n_tok), F32),
                   jax.ShapeDtypeStruct((TOP_K, n_tok), I32),
                   jax.ShapeDtypeStruct((1, n_blk_pad), I32),
                   jax.ShapeDtypeStruct((N_EXPERTS, 1), I32),
                   jax.ShapeDtypeStruct((N_EXPERTS, 1), I32),
                   jax.ShapeDtypeStruct((1, 1), I32)],
        compiler_params=pltpu.CompilerParams(dimension_semantics=("arbitrary",), vmem_limit_bytes=VMEM_LIMIT_BYTES),
        name="route",
    )(*lgts)


def _dispatch_body(cnt_ref, pstart_ref, nused_ref, dest_ref, xp_ref, xs_ref, out_ref, zbuf, sem, *,
                   n_prompt_tiles, n_blk):
    i = pl.program_id(0)

    @pl.when(i == 0)
    def _():
        zbuf[...] = jnp.zeros_like(zbuf)
        sizes = [1 << b for b in reversed(range(ROW_BLOCK.bit_length() - 1))]

        def pad_copies(e):
            cnt = cnt_ref[e]
            first = pstart_ref[e] + cnt
            n_pad = (-cnt) & (ROW_BLOCK - 1)
            for size in sizes:
                start = first + (n_pad & ~(2 * size - 1))
                dst = out_ref.at[pl.ds(pl.multiple_of(start * ROW_TILE, ROW_TILE), size * ROW_TILE), :]
                yield (n_pad & size) != 0, pltpu.make_async_copy(zbuf.at[0:size * ROW_TILE, :], dst, sem)

        for e in range(N_EXPERTS):
            for needed, cp in pad_copies(e):
                pl.when(needed)(cp.start)
        for e in range(N_EXPERTS):
            for needed, cp in pad_copies(e):
                pl.when(needed)(cp.wait)

        def fill_blk(b, c):
            blk_rows = ROW_BLOCK * ROW_TILE
            cp = pltpu.make_async_copy(zbuf, out_ref.at[pl.ds(pl.multiple_of(b * blk_rows, blk_rows), blk_rows), :], sem)
            cp.start()
            cp.wait()
            return c

        lax.fori_loop(nused_ref[0], n_blk, fill_blk, 0)

    def scatter_rows(src):
        def issue(g, c):
            for u in range(ISSUE_UNROLL):
                r = g * ISSUE_UNROLL + u
                for k in range(TOP_K):
                    _tile_copy(src, r, out_ref, dest_ref[0, 0, k * TOK_TILE + r], sem).start(priority=k % 2)
            return c

        lax.fori_loop(0, TOK_TILE // ISSUE_UNROLL, issue, 0)
        for k in range(TOP_K):
            pltpu.make_async_copy(src, out_ref.at[pl.ds(0, TOK_TILE * ROW_TILE), :], sem).wait()

    @pl.when(i < n_prompt_tiles)
    def _():
        scatter_rows(xp_ref)

    @pl.when(i >= n_prompt_tiles)
    def _():
        scatter_rows(xs_ref)


def _dispatch(cnt, pstart, nused, dest_tiles, xn2_p, xn2_s, n_blk):
    tile_rows = TOK_TILE * ROW_TILE
    n_p = xn2_p.shape[0] // tile_rows
    n_s = xn2_s.shape[0] // tile_rows
    assert xn2_p.shape[0] % tile_rows == 0 and xn2_s.shape[0] % tile_rows == 0
    return pl.pallas_call(
        functools.partial(_dispatch_body, n_prompt_tiles=n_p, n_blk=n_blk),
        grid_spec=pltpu.PrefetchScalarGridSpec(
            num_scalar_prefetch=3,
            grid=(n_p + n_s,),
            in_specs=[
                pl.BlockSpec((1, 1, TOP_K * TOK_TILE), lambda i, *_: (i, 0, 0), memory_space=pltpu.SMEM),
                pl.BlockSpec((tile_rows, LANES), lambda i, *_: (jnp.minimum(i, n_p - 1), 0)),
                pl.BlockSpec((tile_rows, LANES), lambda i, *_: (jnp.maximum(i - n_p, 0), 0)),
            ],
            out_specs=pl.BlockSpec(memory_space=pl.ANY),
            scratch_shapes=[pltpu.VMEM((ROW_BLOCK * ROW_TILE, LANES), F32), pltpu.SemaphoreType.DMA],
        ),
        out_shape=jax.ShapeDtypeStruct((n_blk * ROW_BLOCK * ROW_TILE, LANES), F32),
        compiler_params=pltpu.CompilerParams(dimension_semantics=("arbitrary",), vmem_limit_bytes=VMEM_LIMIT_BYTES),
        name="dispatch",
    )(cnt, pstart, nused, dest_tiles, xn2_p, xn2_s)


def _ffn_body(blk_ref, nused_ref, runend_ref, x_ref, wgu_hbm, bgu_ref, wd_hbm, bd_ref, o_ref, wgu_f32, wd_f32,
              wgu_bf, wd_bf, sems, slot_ref):
    i = pl.program_id(0)
    last = nused_ref[0] - 1
    expert = blk_ref[jnp.minimum(i, last)]
    prev_expert = blk_ref[jnp.minimum(jnp.maximum(i - 1, 0), last)]

    def weight_copies(e, slot):
        return (pltpu.make_async_copy(wgu_hbm.at[e], wgu_f32.at[slot], sems.at[slot]),
                pltpu.make_async_copy(wd_hbm.at[e], wd_f32.at[slot], sems.at[slot]))

    @pl.when(i == 0)
    def _():
        slot_ref[0] = 0
        for cp in weight_copies(expert, 0):
            cp.start()

    @pl.when(jnp.logical_or(i == 0, expert != prev_expert))
    def _():
        slot = slot_ref[0]
        for cp in weight_copies(expert, slot):
            cp.wait()
        next_first = runend_ref[expert]

        @pl.when(next_first <= last)
        def _():
            for cp in weight_copies(blk_ref[jnp.minimum(next_first, last)], 1 - slot):
                cp.start()

        wgu_bf[...] = wgu_f32[slot].astype(BF16)
        wd_bf[...] = wd_f32[slot].astype(BF16)
        slot_ref[0] = 1 - slot

    @pl.when(i < nused_ref[0])
    def _():
        x = _load_row_tiles(x_ref, ROW_BLOCK).astype(BF16)
        gu = jnp.dot(x, wgu_bf[...], preferred_element_type=F32) + bgu_ref[0]
        gate = jnp.minimum(gu[:, 0:D_EXPERT], SWIGLU_LIMIT)
        up = jnp.clip(gu[:, D_EXPERT:2 * D_EXPERT], -SWIGLU_LIMIT, SWIGLU_LIMIT)
        act = (up + 1.0) * (gate * jax.nn.sigmoid(SWIGLU_ALPHA * gate))
        out = jnp.dot(act.astype(BF16), wd_bf[...], preferred_element_type=F32) + bd_ref[0]
        _store_row_tiles(o_ref, out, ROW_BLOCK)

    @pl.when(i >= nused_ref[0])
    def _():
        o_ref[...] = jnp.zeros_like(o_ref)


def _ffn(blk_exp, nused, run_end, xs, wgu, bgu, wd, bd):
    blk_rows = ROW_BLOCK * ROW_TILE
    n_blk = xs.shape[0] // blk_rows
    row_map = lambda i, blk, nu, re: (jnp.minimum(i, nu[0] - 1), 0)
    exp_map = lambda i, blk, nu, re: (blk[jnp.minimum(i, nu[0] - 1)], 0, 0)
    return pl.pallas_call(
        _ffn_body,
        grid_spec=pltpu.PrefetchScalarGridSpec(
            num_scalar_prefetch=3,
            grid=(n_blk,),
            in_specs=[
                pl.BlockSpec((blk_rows, LANES), row_map),
                pl.BlockSpec(memory_space=pl.ANY),
                pl.BlockSpec((1, 1, 2 * D_EXPERT), exp_map),
                pl.BlockSpec(memory_space=pl.ANY),
                pl.BlockSpec((1, 1, D_MODEL), exp_map),
            ],
            out_specs=pl.BlockSpec((blk_rows, LANES), lambda i, blk, nu, re: (i, 0)),
            scratch_shapes=[pltpu.VMEM((2, D_MODEL, 2 * D_EXPERT), F32), pltpu.VMEM((2, D_EXPERT, D_MODEL), F32),
                            pltpu.VMEM((D_MODEL, 2 * D_EXPERT), BF16), pltpu.VMEM((D_EXPERT, D_MODEL), BF16),
                            pltpu.SemaphoreType.DMA((2,)), pltpu.SMEM((1,), I32)],
        ),
        out_shape=jax.ShapeDtypeStruct(xs.shape, F32),
        compiler_params=pltpu.CompilerParams(dimension_semantics=("arbitrary",), vmem_limit_bytes=VMEM_LIMIT_BYTES),
        name="ffn",
    )(blk_exp, nused, run_end, xs, wgu, bgu, wd, bd)


def _combine_body(dnext_ref, dcur_ref, ys_ref, x1_ref, gate_ref, gf_ref, y_ref, buf, sems, *, n_tiles):
    i = pl.program_id(0)
    cur = lax.rem(i, 2)
    n_groups = TOK_TILE // COMBINE_GROUP

    def issue_group(dref, slot, g):
        for u in range(COMBINE_GROUP):
            r = g * COMBINE_GROUP + u
            for k in range(TOP_K):
                _tile_copy(ys_ref, dref[0, 0, k * TOK_TILE + r], buf.at[slot, k], r, sems.at[slot]).start(priority=k % 2)

    def finish_group(g):
        base = pl.multiple_of(g * COMBINE_GROUP, COMBINE_GROUP)
        gates = gate_ref[pl.ds(base, COMBINE_GROUP), :]
        moe = None
        for k in range(TOP_K):
            rows = jnp.concatenate(
                [buf[cur, k, pl.ds(base * ROW_TILE + j, COMBINE_GROUP, stride=ROW_TILE), :] for j in range(ROW_TILE)],
                axis=1)
            term = rows * gates[:, k:k + 1]
            moe = term if moe is None else moe + term
        x2 = x1_ref[pl.ds(base, COMBINE_GROUP), :] + moe
        ms = jnp.mean(x2 * x2, axis=-1, keepdims=True)
        y_ref[pl.ds(base, COMBINE_GROUP), :] = x2 * lax.rsqrt(ms + RMS_EPS) * gf_ref[...]

    @pl.when(i == 0)
    def _():
        def first(g, c):
            issue_group(dcur_ref, 0, g)
            return c

        lax.fori_loop(0, n_groups, first, 0)

    for k in range(TOP_K):
        pltpu.make_async_copy(ys_ref.at[pl.ds(0, TOK_TILE * ROW_TILE), :], buf.at[cur, k], sems.at[cur]).wait()

    @pl.when(i + 1 < n_tiles)
    def _():
        def both(g, c):
            issue_group(dnext_ref, 1 - cur, g)
            finish_group(g)
            return c

        lax.fori_loop(0, n_groups, both, 0)

    @pl.when(i + 1 >= n_tiles)
    def _():
        def last(g, c):
            finish_group(g)
            return c

        lax.fori_loop(0, n_groups, last, 0)


def _combine(dest_tiles, ys, x1, gates_t, gf, tile0):
    n = x1.shape[0]
    assert n % TOK_TILE == 0
    n_tiles = n // TOK_TILE
    dest_spec = lambda ahead: pl.BlockSpec(
        (1, 1, TOP_K * TOK_TILE), lambda i: (tile0 + jnp.minimum(i + ahead, n_tiles - 1), 0, 0), memory_space=pltpu.SMEM)
    return pl.pallas_call(
        functools.partial(_combine_body, n_tiles=n_tiles),
        grid=(n_tiles,),
        in_specs=[
            dest_spec(1), dest_spec(0),
            pl.BlockSpec(memory_space=pl.ANY),
            pl.BlockSpec((TOK_TILE, D_MODEL), lambda i: (i, 0)),
            pl.BlockSpec((TOK_TILE, TOP_K), lambda i: (tile0 + i, 0)),
            pl.BlockSpec((1, D_MODEL), lambda i: (0, 0)),
        ],
        out_specs=pl.BlockSpec((TOK_TILE, D_MODEL), lambda i: (i, 0)),
        out_shape=jax.ShapeDtypeStruct((n, D_MODEL), F32),
        scratch_shapes=[pltpu.VMEM((2, TOP_K, TOK_TILE * ROW_TILE, LANES), F32), pltpu.SemaphoreType.DMA((2,))],
        compiler_params=pltpu.CompilerParams(dimension_semantics=("arbitrary",), vmem_limit_bytes=VMEM_LIMIT_BYTES),
        name="combine",
    )(dest_tiles, dest_tiles, ys, x1, gates_t, gf)


def _pack_gate_weights(w_rg, w_ig):
    def bd(w):
        w = w.reshape(D_RNN // MXU_DIM, GATE_PACK, LRU_BLOCK, LRU_BLOCK)
        eye = jnp.eye(GATE_PACK, dtype=w.dtype)
        return jnp.einsum('jpik,pq->jpiqk', w, eye).reshape(D_RNN // MXU_DIM, MXU_DIM, MXU_DIM)
    return jnp.concatenate([bd(w_rg), bd(w_ig)], axis=-1).astype(BF16)


def kernel(x_prompt, x_sample, state_conv, state_h, norm1_g, w_in, conv_w, conv_b, w_rg, b_rg, w_ig, b_ig, lru_lambda, ln_v_g, ln_v_b, w_s, b_s, w_proj_a, w_proj_b, w_out, norm2_g, w_router, b_router, w_gate_up, b_gate_up, w_down, b_down, final_norm_g):
    assert norm1_g.shape[0] == 1, "single layer"
    B, S, _ = x_prompt.shape
    Bs, Ss, _ = x_sample.shape
    row = lambda p: p.reshape(1, -1).astype(F32)
    wts = (
        row(norm1_g[0]), w_in[0].astype(BF16), conv_w[0], row(conv_b[0]), _pack_gate_weights(w_rg[0], w_ig[0]),
        row(b_rg[0]), row(b_ig[0]), row(lru_lambda[0]), row(ln_v_g[0]), row(ln_v_b[0]), w_s[0], b_s[0].T,
        w_proj_a[0].astype(BF16), w_proj_b[0].astype(BF16), w_out[0].astype(BF16), row(norm2_g[0]),
        w_router[0].T.astype(BF16), b_router[0].reshape(N_EXPERTS, 1),
    )
    x1_p, xn2_p, lgt_p, convl_p, hl_p = _mixer(
        x_prompt, jnp.zeros((B, SUBLANES, D_RNN), F32), jnp.zeros((B, 1, D_RNN), F32), wts,
        nseq=1, seg=min(MIX_ROWS, S), start_zero=True, want_v=False)
    conv0_s = jnp.pad(state_conv[0], ((0, 0), (SUBLANES - (CONV_WIDTH - 1), 0), (0, 0)))
    x1_s, xn2_s, lgt_s, convl_s, hl_s, v_s = _mixer(
        x_sample, conv0_s, state_h[0][:, None, :], wts,
        nseq=Bs, seg=Ss, start_zero=False, want_v=True)

    n_p, n_s = B * S, Bs * Ss
    n_tok = n_p + n_s
    n_blk = -(-(n_tok * TOP_K + N_EXPERTS * (ROW_BLOCK - 1)) // ROW_BLOCK)
    _, gate, dest, blk_exp, cnt, pstart, nused = _route((lgt_p, lgt_s), n_blk)
    assert n_tok % TOK_TILE == 0
    n_tiles = n_tok // TOK_TILE
    dest_tiles = dest.reshape(TOP_K, n_tiles, TOK_TILE).transpose(1, 0, 2).reshape(n_tiles, 1, TOP_K * TOK_TILE)
    gates_t = gate.T
    cnt, pstart, nused, blk_exp = cnt.reshape(-1), pstart.reshape(-1), nused.reshape(-1), blk_exp.reshape(-1)

    xs = _dispatch(cnt, pstart, nused, dest_tiles, xn2_p, xn2_s, n_blk)
    run_end = (pstart + cnt + (ROW_BLOCK - 1)) // ROW_BLOCK
    ys = _ffn(blk_exp, nused, run_end, xs, w_gate_up[0], b_gate_up[0][:, None, :], w_down[0], b_down[0][:, None, :])
    gf = row(final_norm_g)
    y_p = _combine(dest_tiles, ys, x1_p.reshape(n_p, D_MODEL), gates_t, gf, 0).reshape(B, S, D_MODEL)
    y_s = _combine(dest_tiles, ys, x1_s.reshape(n_s, D_MODEL), gates_t, gf, n_p // TOK_TILE).reshape(Bs, Ss, D_MODEL)

    keep = slice(SUBLANES - (CONV_WIDTH - 1), SUBLANES)
    return (y_p, y_s, convl_p[None, :, keep, :], hl_p.reshape(1, B, D_RNN), convl_s[None, :, keep, :],
            hl_s.reshape(1, Bs, D_RNN), v_s[None])
```

```python
import functools

import jax
import jax.numpy as jnp
from jax import lax
from jax.experimental import pallas as pl
from jax.experimental.pallas import tpu as pltpu

F32 = jnp.float32
BF16 = jnp.bfloat16
I32 = jnp.int32

D_MODEL = 1024
D_RNN = D_MODEL
N_LRU_BLOCKS = 16
LRU_BLOCK = D_RNN // N_LRU_BLOCKS
CONV_WIDTH = 4
LRU_C = 8.0
MLP_CHUNK = 128
N_MLP_GROUPS = 4
MLP_GROUP_DIM = D_MODEL // 8
D_MLP = N_MLP_GROUPS * MLP_GROUP_DIM
N_IN = 2 * D_RNN + 2 * D_MLP + 2 * D_MODEL
N_EXPERTS = 32
TOP_K = 4
D_EXPERT = D_MODEL
SWIGLU_LIMIT = 7.0
SWIGLU_ALPHA = 1.702
RMS_EPS = 1e-6
LN_EPS = 1e-5

SUBLANES = 8
LANES = 128
MXU_DIM = 256
ROW_TILE = D_MODEL // LANES
VMEM_LIMIT_BYTES = 56 * 1024 * 1024

MIX_ROWS = 256
PROJ_CUTS = (512, 3584, 4608)
ROUTE_CHUNK = 256
ROW_BLOCK = 512
TOK_TILE = 256
ISSUE_UNROLL = 8
COMBINE_GROUP = 64
GATE_PACK = MXU_DIM // LRU_BLOCK


def _const_spec(shape):
    nd = len(shape)
    return pl.BlockSpec(shape, lambda *_: (0,) * nd, pipeline_mode=pl.Buffered(1))


def _store_row_tiles(ref, val, n):
    for j in range(ROW_TILE):
        ref[pl.ds(j, n, stride=ROW_TILE), :] = val[:, j * LANES:(j + 1) * LANES]


def _load_row_tiles(ref, n):
    return jnp.concatenate([ref[pl.ds(j, n, stride=ROW_TILE), :] for j in range(ROW_TILE)], axis=1)


def _row_tile(ref, row):
    start = row * ROW_TILE
    if not isinstance(start, int):
        start = pl.multiple_of(start, ROW_TILE)
    return ref.at[pl.ds(start, ROW_TILE), :]


def _tile_copy(src, src_row, dst, dst_row, sem):
    return pltpu.make_async_copy(_row_tile(src, src_row), _row_tile(dst, dst_row), sem)


def _norm1(x, g1_ref):
    ms = jnp.mean(x * x, axis=-1, keepdims=True)
    return (x * lax.rsqrt(ms + RMS_EPS) * g1_ref[...]).astype(BF16)


def _project(x, g1_ref, win_ref):
    return jnp.dot(_norm1(x, g1_ref), win_ref[...], preferred_element_type=F32)


def _restart_state(conv0_ref, h0_ref, state):
    xa_buf, h_tiles, _, _ = state
    nseq = xa_buf.shape[0]
    xa_buf[:, 0:SUBLANES, :] = conv0_ref[...]
    _store_row_tiles(h_tiles, h0_ref[...].reshape(nseq, D_RNN), nseq)


def _mix_tile(x, z, s, wrefs, orefs, state, *, nseq, seg, mc, start_zero, want_v, fill_mxu=lambda slot: None):
    (convw_ref, convb_ref, wg_ref, brg_ref, big_ref, lam_ref, lng_ref, lnb_ref, ws_ref, bst_ref, wpa_ref, wpb_ref,
     wout_ref, g2_ref, wrt_ref, brt_ref) = wrefs
    if want_v:
        x1_ref, xn2_ref, lgt_ref, convl_ref, hl_ref, v_ref = orefs
    else:
        x1_ref, xn2_ref, lgt_ref, convl_ref, hl_ref = orefs
        v_ref = None
    xa_buf, h_tiles, scan_a, scan_b = state
    rows = nseq * seg
    xa = z[:, 0:D_RNN]
    ga = z[:, D_RNN:2 * D_RNN]
    zu = z[:, 2 * D_RNN:2 * D_RNN + D_MLP]
    zv = z[:, 2 * D_RNN + D_MLP:2 * D_RNN + 2 * D_MLP]
    gate_a = z[:, 2 * D_RNN + 2 * D_MLP:2 * D_RNN + 2 * D_MLP + D_MODEL]
    gate_b = z[:, 2 * D_RNN + 2 * D_MLP + D_MODEL:N_IN]

    cw = convw_ref[...]
    xc_parts = []
    for q in range(nseq):
        xa_buf[q, SUBLANES:SUBLANES + seg, :] = xa[q * seg:(q + 1) * seg]
        acc = convb_ref[...] + xa[q * seg:(q + 1) * seg] * cw[CONV_WIDTH - 1:CONV_WIDTH]
        for k in range(CONV_WIDTH - 1):
            off = SUBLANES - (CONV_WIDTH - 1) + k
            acc = acc + xa_buf[q, off:off + seg, :] * cw[k:k + 1]
        xc_parts.append(acc)
        xa_buf[q, 0:SUBLANES, :] = xa_buf[q, seg:seg + SUBLANES, :]
    xc = xc_parts[0] if nseq == 1 else jnp.concatenate(xc_parts, axis=0)
    convl_ref[...] = xa_buf[:, 0:SUBLANES, :]

    u = jax.nn.gelu(zu)
    gv = jax.nn.gelu(zv)
    mu = jnp.mean(gv, axis=-1, keepdims=True)
    vc = gv - mu
    var = jnp.mean(vc * vc, axis=-1, keepdims=True)
    v = vc * lax.rsqrt(var + LN_EPS) * lng_ref[...] + lnb_ref[...]
    if want_v:
        v_ref[...] = v.reshape(nseq, seg, D_MLP)
    vb = v.astype(BF16)

    xcb = xc.astype(BF16)
    r_parts, i_parts = [], []
    for j in range(D_RNN // MXU_DIM):
        gj = jnp.dot(xcb[:, j * MXU_DIM:(j + 1) * MXU_DIM], wg_ref[j], preferred_element_type=F32)
        r_parts.append(gj[:, 0:MXU_DIM])
        i_parts.append(gj[:, MXU_DIM:2 * MXU_DIM])
    fill_mxu(0)

    tri = lax.broadcasted_iota(I32, (mc, mc), 0) >= lax.broadcasted_iota(I32, (mc, mc), 1)
    ws_tri = [jnp.where(tri, ws_ref[g][0:mc, 0:mc], 0.0).astype(BF16) for g in range(N_MLP_GROUPS)]
    bst = bst_ref[...]
    mix_rows = []
    for c in range(rows // mc):
        cols = []
        for g in range(N_MLP_GROUPS):
            vg = vb[c * mc:(c + 1) * mc, g * MLP_GROUP_DIM:(g + 1) * MLP_GROUP_DIM]
            mg = jnp.dot(ws_tri[g], vg, preferred_element_type=F32) + bst[0:mc, g:g + 1]
            cols.append(mg)
        mix_rows.append(jnp.concatenate(cols, axis=1))
    vmix = mix_rows[0] if len(mix_rows) == 1 else jnp.concatenate(mix_rows, axis=0)

    r = jax.nn.sigmoid(jnp.concatenate(r_parts, axis=1) + brg_ref[...])
    ig = jax.nn.sigmoid(jnp.concatenate(i_parts, axis=1) + big_ref[...])
    lam = lam_ref[...]
    softplus_neg = jnp.maximum(-lam, 0.0) + jnp.log1p(jnp.exp(-jnp.abs(lam)))
    log_a = (-LRU_C) * r * softplus_neg
    a = jnp.exp(log_a)
    mult = jnp.sqrt(jnp.maximum(1.0 - a * a, 0.0))
    if start_zero:
        row = lax.broadcasted_iota(I32, (rows, 1), 0)
        mult = jnp.where(jnp.logical_and(row == 0, s == 0), 1.0, mult)
    bterm = mult * ig * xc

    _store_row_tiles(scan_a, a, rows)
    _store_row_tiles(scan_b, bterm, rows)
    for q in range(nseq):
        hq = h_tiles[q * ROW_TILE:(q + 1) * ROW_TILE, :]
        for t in range(seg):
            r = (q * seg + t) * ROW_TILE
            hq = scan_a[r:r + ROW_TILE, :] * hq + scan_b[r:r + ROW_TILE, :]
            scan_b[r:r + ROW_TILE, :] = hq
        h_tiles[q * ROW_TILE:(q + 1) * ROW_TILE, :] = hq
    h = _load_row_tiles(scan_b, rows)
    hl_ref[...] = _load_row_tiles(h_tiles, nseq).reshape(nseq, 1, D_RNN)

    ya = jnp.dot((h * jax.nn.gelu(ga)).astype(BF16), wpa_ref[...], preferred_element_type=F32)
    yb = jnp.dot((u * vmix).astype(BF16), wpb_ref[...], preferred_element_type=F32)
    fill_mxu(1)

    merged = jax.nn.sigmoid(gate_a) * ya + jax.nn.sigmoid(gate_b) * yb
    x1 = x + jnp.dot(merged.astype(BF16), wout_ref[...], preferred_element_type=F32)
    fill_mxu(2)
    x1_ref[...] = x1.reshape(nseq, seg, D_MODEL)

    ms2 = jnp.mean(x1 * x1, axis=-1, keepdims=True)
    xn2 = x1 * lax.rsqrt(ms2 + RMS_EPS) * g2_ref[...]
    _store_row_tiles(xn2_ref, xn2, rows)
    lgt = lax.dot_general(wrt_ref[...], xn2.astype(BF16), (((1,), (1,)), ((), ())), preferred_element_type=F32)
    lgt_ref[...] = lgt + brt_ref[...]


N_MIX_WEIGHTS = 18


def _mixer_whole_body(x_ref, conv0_ref, h0_ref, g1_ref, win_ref, *rest, n_out, **static):
    wrefs = rest[:N_MIX_WEIGHTS - 2]
    orefs = rest[N_MIX_WEIGHTS - 2:N_MIX_WEIGHTS - 2 + n_out]
    state = rest[N_MIX_WEIGHTS - 2 + n_out:]
    x = x_ref[...].reshape(static["nseq"] * static["seg"], D_MODEL)
    _restart_state(conv0_ref, h0_ref, state)
    _mix_tile(x, _project(x, g1_ref, win_ref), 0, wrefs, orefs, state, **static)


def _mixer_pipe_body(xnext_ref, x_ref, conv0_ref, h0_ref, g1_ref, win_ref, *rest, n_out, n_s, **static):
    wrefs = rest[:N_MIX_WEIGHTS - 2]
    orefs = rest[N_MIX_WEIGHTS - 2:N_MIX_WEIGHTS - 2 + n_out]
    *state, z_even, z_odd = rest[N_MIX_WEIGHTS - 2 + n_out:]
    t = pl.program_id(0)
    rows = static["seg"]
    s_prev = lax.rem(jnp.maximum(t - 1, 0), n_s)

    @pl.when(t == 0)
    def _():
        z_odd[...] = jnp.zeros_like(z_odd)

    @pl.when(s_prev == 0)
    def _():
        _restart_state(conv0_ref, h0_ref, state)

    def step(z_new, z_done):
        xn = _norm1(xnext_ref[...].reshape(rows, D_MODEL), g1_ref)

        def project(lo, hi):
            z_new[:, lo:hi] = jnp.dot(xn, win_ref[:, lo:hi], preferred_element_type=F32)

        cuts = (0,) + PROJ_CUTS + (N_IN,)
        project(cuts[0], cuts[1])
        _mix_tile(x_ref[...].reshape(rows, D_MODEL), z_done, s_prev, wrefs, orefs, state,
                  fill_mxu=lambda slot: project(cuts[slot + 1], cuts[slot + 2]), **static)

    @pl.when(lax.rem(t, 2) == 0)
    def _():
        step(z_even, z_odd)

    @pl.when(lax.rem(t, 2) == 1)
    def _():
        step(z_odd, z_even)


def _mixer(x, conv0, h0, wts, *, nseq, seg, start_zero, want_v):
    B, S, _ = x.shape
    assert len(wts) == N_MIX_WEIGHTS
    assert B % nseq == 0 and S % seg == 0 and seg % SUBLANES == 0
    assert nseq == 1 or (seg == S and nseq == B)
    mc = min(seg, MLP_CHUNK)
    assert seg % mc == 0
    rows = nseq * seg
    n_s = S // seg
    n_tiles = (B // nseq) * n_s
    pipelined = nseq == 1
    done = (lambda t: jnp.maximum(t - 1, 0)) if pipelined else (lambda t: t)
    seq_spec = lambda w, tile: pl.BlockSpec((nseq, seg, w), lambda t: (tile(t) // n_s, tile(t) % n_s, 0))
    state_spec = lambda r: pl.BlockSpec((nseq, r, D_RNN), lambda t: (done(t) // n_s, 0, 0))
    in_specs = [seq_spec(D_MODEL, done), state_spec(SUBLANES), state_spec(1)] + [_const_spec(w.shape) for w in wts]
    out_shape = [
        jax.ShapeDtypeStruct((B, S, D_MODEL), F32),
        jax.ShapeDtypeStruct((B * S * ROW_TILE, LANES), F32),
        jax.ShapeDtypeStruct((N_EXPERTS, B * S), F32),
        jax.ShapeDtypeStruct((B, SUBLANES, D_RNN), F32),
        jax.ShapeDtypeStruct((B, 1, D_RNN), F32),
    ]
    out_specs = [
        seq_spec(D_MODEL, done), pl.BlockSpec((rows * ROW_TILE, LANES), lambda t: (done(t), 0)),
        pl.BlockSpec((N_EXPERTS, rows), lambda t: (0, done(t))),
        state_spec(SUBLANES), state_spec(1),
    ]
    if want_v:
        out_shape.append(jax.ShapeDtypeStruct((B, S, D_MLP), F32))
        out_specs.append(seq_spec(D_MLP, done))
    static = dict(n_out=len(out_shape), nseq=nseq, seg=seg, mc=mc, start_zero=start_zero, want_v=want_v)
    scratch = [pltpu.VMEM((nseq, seg + SUBLANES, D_RNN), F32), pltpu.VMEM((nseq * ROW_TILE, LANES), F32),
               pltpu.VMEM((rows * ROW_TILE, LANES), F32), pltpu.VMEM((rows * ROW_TILE, LANES), F32)]
    if pipelined:
        body = functools.partial(_mixer_pipe_body, n_s=n_s, **static)
        grid = (n_tiles + 1,)
        in_specs = [seq_spec(D_MODEL, lambda t: jnp.minimum(t, n_tiles - 1))] + in_specs
        scratch += [pltpu.VMEM((rows, N_IN), F32), pltpu.VMEM((rows, N_IN), F32)]
        operands = (x, x, conv0, h0, *wts)
    else:
        body = functools.partial(_mixer_whole_body, **static)
        grid = (n_tiles,)
        operands = (x, conv0, h0, *wts)
    return pl.pallas_call(
        body,
        grid=grid,
        in_specs=in_specs,
        out_specs=out_specs,
        out_shape=out_shape,
        scratch_shapes=scratch,
        compiler_params=pltpu.CompilerParams(dimension_semantics=("arbitrary",), vmem_limit_bytes=VMEM_LIMIT_BYTES),
        name="mixer",
    )(*operands)


def _route_body(*refs, n_toks, n_blk_pad):
    lgt_refs = refs[:len(n_toks)]
    e_ref, gate_ref, dest_ref, blk_ref, cnt_ref, pstart_ref, nused_ref = refs[len(n_toks):]
    ch = ROUTE_CHUNK
    n_chunks = sum(n_toks) // ch
    eiota = lax.broadcasted_iota(I32, (N_EXPERTS, ch), 0)
    upper = (lax.broadcasted_iota(I32, (ch, ch), 0) < lax.broadcasted_iota(I32, (ch, ch), 1)).astype(BF16)

    def rank_chunk(lgt_ref, first_chunk, c, counts):
        off = pl.multiple_of((first_chunk + c) * ch, ch)
        work = lgt_ref[:, pl.ds(pl.multiple_of(c * ch, ch), ch)]
        sel = jnp.zeros((N_EXPERTS, ch), F32)
        es, vs = [], []
        for _ in range(TOP_K):
            m = jnp.max(work, axis=0, keepdims=True)
            idx = jnp.min(jnp.where(work == m, eiota, N_EXPERTS), axis=0, keepdims=True)
            hit = eiota == idx
            es.append(idx)
            vs.append(m)
            work = jnp.where(hit, -jnp.inf, work)
            sel = jnp.where(hit, 1.0, sel)
        exps = [jnp.exp(v - vs[0]) for v in vs]
        inv = 1.0 / (exps[0] + exps[1] + exps[2] + exps[3])
        before = jnp.dot(sel.astype(BF16), upper, preferred_element_type=F32) + counts
        for k in range(TOP_K):
            rank = jnp.sum(jnp.where(eiota == es[k], before, 0.0), axis=0, keepdims=True)
            e_ref[k:k + 1, pl.ds(off, ch)] = es[k]
            gate_ref[k:k + 1, pl.ds(off, ch)] = exps[k] * inv
            dest_ref[k:k + 1, pl.ds(off, ch)] = rank.astype(I32)
        return counts + jnp.sum(sel, axis=1, keepdims=True)

    counts = jnp.zeros((N_EXPERTS, 1), F32)
    first_chunk = 0
    for lgt_ref, n in zip(lgt_refs, n_toks):
        counts = lax.fori_loop(0, n // ch, functools.partial(rank_chunk, lgt_ref, first_chunk), counts)
        first_chunk += n // ch
    counts = counts.astype(I32)
    shift = ROW_BLOCK.bit_length() - 1
    padded = lax.shift_left(lax.shift_right_logical(counts + (ROW_BLOCK - 1), shift), shift)
    col = lax.broadcasted_iota(I32, (N_EXPERTS, 1), 0)
    pad_end = jnp.zeros((N_EXPERTS, 1), I32)
    for e in range(N_EXPERTS):
        tot = jnp.sum(jnp.where(col <= e, padded, 0), axis=0, keepdims=True)
        pad_end = jnp.where(col == e, tot, pad_end)
    pad_start = pad_end - padded
    cnt_ref[...] = counts
    pstart_ref[...] = pad_start
    nused_ref[...] = lax.shift_right_logical(jnp.max(pad_end, axis=0, keepdims=True), shift)
    blk_row = lax.broadcasted_iota(I32, (N_EXPERTS, n_blk_pad), 1) * ROW_BLOCK
    blk_ref[...] = jnp.minimum(jnp.sum((pad_end <= blk_row).astype(I32), axis=0, keepdims=True), N_EXPERTS - 1)

    def place_chunk(c, carry):
        off = pl.multiple_of(c * ch, ch)
        for k in range(TOP_K):
            ek = e_ref[k:k + 1, pl.ds(off, ch)]
            base = jnp.sum(jnp.where(eiota == ek, pad_start, 0), axis=0, keepdims=True)
            dest_ref[k:k + 1, pl.ds(off, ch)] = dest_ref[k:k + 1, pl.ds(off, ch)] + base
        return carry

    lax.fori_loop(0, n_chunks, place_chunk, 0)


def _route(lgts, n_blk):
    n_toks = tuple(lgt.shape[1] for lgt in lgts)
    assert all(n % ROUTE_CHUNK == 0 for n in n_toks)
    n_tok = sum(n_toks)
    n_blk_pad = -(-n_blk // LANES) * LANES
    full = lambda shape: pl.BlockSpec(shape, lambda i: (0,) * len(shape))
    return pl.pallas_call(
        functools.partial(_route_body, n_toks=n_toks, n_blk_pad=n_blk_pad),
        grid=(1,),
        in_specs=[full(lgt.shape) for lgt in lgts],
        out_specs=[full((TOP_K, n_tok)), full((TOP_K, n_tok)), full((TOP_K, n_tok)), full((1, n_blk_pad)),
                   full((N_EXPERTS, 1)), full((N_EXPERTS, 1)), full((1, 1))],
        out_shape=[jax.ShapeDtypeStruct((TOP_K, n_tok), I32),
                   jax.ShapeDtypeStruct((TOP_K, n_tok), F32),
                   jax.ShapeDtypeStruct((TOP_K, n_tok), I32),
                   jax.ShapeDtypeStruct((1, n_blk_pad), I32),
                   jax.ShapeDtypeStruct((N_EXPERTS, 1), I32),
                   jax.ShapeDtypeStruct((N_EXPERTS, 1), I32),
                   jax.ShapeDtypeStruct((1, 1), I32)],
        compiler_params=pltpu.CompilerParams(dimension_semantics=("arbitrary",), vmem_limit_bytes=VMEM_LIMIT_BYTES),
        name="route",
    )(*lgts)


def _dispatch_body(cnt_ref, pstart_ref, nused_ref, dest_ref, xp_ref, xs_ref, out_ref, zbuf, sem, *,
                   n_prompt_tiles, n_blk):
    i = pl.program_id(0)

    @pl.when(i == 0)
    def _():
        zbuf[...] = jnp.zeros_like(zbuf)
        sizes = [1 << b for b in reversed(range(ROW_BLOCK.bit_length() - 1))]

        def pad_copies(e):
            cnt = cnt_ref[e]
            first = pstart_ref[e] + cnt
            n_pad = (-cnt) & (ROW_BLOCK - 1)
            for size in sizes:
                start = first + (n_pad & ~(2 * size - 1))
                dst = out_ref.at[pl.ds(pl.multiple_of(start * ROW_TILE, ROW_TILE), size * ROW_TILE), :]
                yield (n_pad & size) != 0, pltpu.make_async_copy(zbuf.at[0:size * ROW_TILE, :], dst, sem)

        for e in range(N_EXPERTS):
            for needed, cp in pad_copies(e):
                pl.when(needed)(cp.start)
        for e in range(N_EXPERTS):
            for needed, cp in pad_copies(e):
                pl.when(needed)(cp.wait)

        def blk_copy(b):
            blk_rows = ROW_BLOCK * ROW_TILE
            return pltpu.make_async_copy(
                zbuf, out_ref.at[pl.ds(pl.multiple_of(b * blk_rows, blk_rows), blk_rows), :], sem)

        def fill_blk(b, c):
            blk_copy(b).start()
            return c

        def drain_blk(b, c):
            blk_copy(b).wait()
            return c

        lax.fori_loop(nused_ref[0], n_blk, fill_blk, 0)
        lax.fori_loop(nused_ref[0], n_blk, drain_blk, 0)

    def scatter_rows(src):
        def issue(g, c):
            for u in range(ISSUE_UNROLL):
                r = g * ISSUE_UNROLL + u
                for k in range(TOP_K):
                    _tile_copy(src, r, out_ref, dest_ref[0, 0, k * TOK_TILE + r], sem).start(priority=k % 2)
            return c

        lax.fori_loop(0, TOK_TILE // ISSUE_UNROLL, issue, 0)
        for k in range(TOP_K):
            pltpu.make_async_copy(src, out_ref.at[pl.ds(0, TOK_TILE * ROW_TILE), :], sem).wait()

    @pl.when(i < n_prompt_tiles)
    def _():
        scatter_rows(xp_ref)

    @pl.when(i >= n_prompt_tiles)
    def _():
        scatter_rows(xs_ref)


def _dispatch(cnt, pstart, nused, dest_tiles, xn2_p, xn2_s, n_blk):
    tile_rows = TOK_TILE * ROW_TILE
    n_p = xn2_p.shape[0] // tile_rows
    n_s = xn2_s.shape[0] // tile_rows
    assert xn2_p.shape[0] % tile_rows == 0 and xn2_s.shape[0] % tile_rows == 0
    return pl.pallas_call(
        functools.partial(_dispatch_body, n_prompt_tiles=n_p, n_blk=n_blk),
        grid_spec=pltpu.PrefetchScalarGridSpec(
            num_scalar_prefetch=3,
            grid=(n_p + n_s,),
            in_specs=[
                pl.BlockSpec((1, 1, TOP_K * TOK_TILE), lambda i, *_: (i, 0, 0), memory_space=pltpu.SMEM),
                pl.BlockSpec((tile_rows, LANES), lambda i, *_: (jnp.minimum(i, n_p - 1), 0)),
                pl.BlockSpec((tile_rows, LANES), lambda i, *_: (jnp.maximum(i - n_p, 0), 0)),
            ],
            out_specs=pl.BlockSpec(memory_space=pl.ANY),
            scratch_shapes=[pltpu.VMEM((ROW_BLOCK * ROW_TILE, LANES), F32), pltpu.SemaphoreType.DMA],
        ),
        out_shape=jax.ShapeDtypeStruct((n_blk * ROW_BLOCK * ROW_TILE, LANES), F32),
        compiler_params=pltpu.CompilerParams(dimension_semantics=("arbitrary",), vmem_limit_bytes=VMEM_LIMIT_BYTES),
        name="dispatch",
    )(cnt, pstart, nused, dest_tiles, xn2_p, xn2_s)


def _ffn_body(blk_ref, nused_ref, runend_ref, x_ref, wgu_hbm, bgu_ref, wd_hbm, bd_ref, o_ref, wgu_f32, wd_f32,
              wgu_bf, wd_bf, sems, slot_ref):
    i = pl.program_id(0)
    last = nused_ref[0] - 1
    expert = blk_ref[jnp.minimum(i, last)]
    prev_expert = blk_ref[jnp.minimum(jnp.maximum(i - 1, 0), last)]

    def weight_copies(e, slot):
        return (pltpu.make_async_copy(wgu_hbm.at[e], wgu_f32.at[slot], sems.at[slot]),
                pltpu.make_async_copy(wd_hbm.at[e], wd_f32.at[slot], sems.at[slot]))

    @pl.when(i == 0)
    def _():
        slot_ref[0] = 0
        for cp in weight_copies(expert, 0):
            cp.start()

    @pl.when(jnp.logical_or(i == 0, expert != prev_expert))
    def _():
        slot = slot_ref[0]
        for cp in weight_copies(expert, slot):
            cp.wait()
        next_first = runend_ref[expert]

        @pl.when(next_first <= last)
        def _():
            for cp in weight_copies(blk_ref[jnp.minimum(next_first, last)], 1 - slot):
                cp.start()

        wgu_bf[...] = wgu_f32[slot].astype(BF16)
        wd_bf[...] = wd_f32[slot].astype(BF16)
        slot_ref[0] = 1 - slot

    @pl.when(i < nused_ref[0])
    def _():
        x = _load_row_tiles(x_ref, ROW_BLOCK).astype(BF16)
        gu = jnp.dot(x, wgu_bf[...], preferred_element_type=F32) + bgu_ref[0]
        gate = jnp.minimum(gu[:, 0:D_EXPERT], SWIGLU_LIMIT)
        up = jnp.clip(gu[:, D_EXPERT:2 * D_EXPERT], -SWIGLU_LIMIT, SWIGLU_LIMIT)
        act = (up + 1.0) * (gate * jax.nn.sigmoid(SWIGLU_ALPHA * gate))
        out = jnp.dot(act.astype(BF16), wd_bf[...], preferred_element_type=F32) + bd_ref[0]
        _store_row_tiles(o_ref, out, ROW_BLOCK)

    @pl.when(i >= nused_ref[0])
    def _():
        o_ref[...] = jnp.zeros_like(o_ref)


def _ffn(blk_exp, nused, run_end, xs, wgu, bgu, wd, bd):
    blk_rows = ROW_BLOCK * ROW_TILE
    n_blk = xs.shape[0] // blk_rows
    row_map = lambda i, blk, nu, re: (jnp.minimum(i, nu[0] - 1), 0)
    exp_map = lambda i, blk, nu, re: (blk[jnp.minimum(i, nu[0] - 1)], 0, 0)
    return pl.pallas_call(
        _ffn_body,
        grid_spec=pltpu.PrefetchScalarGridSpec(
            num_scalar_prefetch=3,
            grid=(n_blk,),
            in_specs=[
                pl.BlockSpec((blk_rows, LANES), row_map),
                pl.BlockSpec(memory_space=pl.ANY),
                pl.BlockSpec((1, 1, 2 * D_EXPERT), exp_map),
                pl.BlockSpec(memory_space=pl.ANY),
                pl.BlockSpec((1, 1, D_MODEL), exp_map),
            ],
            out_specs=pl.BlockSpec((blk_rows, LANES), lambda i, blk, nu, re: (i, 0)),
            scratch_shapes=[pltpu.VMEM((2, D_MODEL, 2 * D_EXPERT), F32), pltpu.VMEM((2, D_EXPERT, D_MODEL), F32),
                            pltpu.VMEM((D_MODEL, 2 * D_EXPERT), BF16), pltpu.VMEM((D_EXPERT, D_MODEL), BF16),
                            pltpu.SemaphoreType.DMA((2,)), pltpu.SMEM((1,), I32)],
        ),
        out_shape=jax.ShapeDtypeStruct(xs.shape, F32),
        compiler_params=pltpu.CompilerParams(dimension_semantics=("arbitrary",), vmem_limit_bytes=VMEM_LIMIT_BYTES),
        name="ffn",
    )(blk_exp, nused, run_end, xs, wgu, bgu, wd, bd)


def _combine_body(dnext_ref, dcur_ref, ys_ref, x1_ref, gate_ref, gf_ref, y_ref, buf, sems, *, n_tiles):
    i = pl.program_id(0)
    cur = lax.rem(i, 2)
    n_groups = TOK_TILE // COMBINE_GROUP

    def issue_group(dref, slot, g):
        for u in range(COMBINE_GROUP):
            r = g * COMBINE_GROUP + u
            for k in range(TOP_K):
                _tile_copy(ys_ref, dref[0, 0, k * TOK_TILE + r], buf.at[slot, k], r, sems.at[slot]).start(priority=k % 2)

    def finish_group(g):
        base = pl.multiple_of(g * COMBINE_GROUP, COMBINE_GROUP)
        gates = gate_ref[pl.ds(base, COMBINE_GROUP), :]
        moe = None
        for k in range(TOP_K):
            rows = jnp.concatenate(
                [buf[cur, k, pl.ds(base * ROW_TILE + j, COMBINE_GROUP, stride=ROW_TILE), :] for j in range(ROW_TILE)],
                axis=1)
            term = rows * gates[:, k:k + 1]
            moe = term if moe is None else moe + term
        x2 = x1_ref[pl.ds(base, COMBINE_GROUP), :] + moe
        ms = jnp.mean(x2 * x2, axis=-1, keepdims=True)
        y_ref[pl.ds(base, COMBINE_GROUP), :] = x2 * lax.rsqrt(ms + RMS_EPS) * gf_ref[...]

    @pl.when(i == 0)
    def _():
        def first(g, c):
            issue_group(dcur_ref, 0, g)
            return c

        lax.fori_loop(0, n_groups, first, 0)

    for k in range(TOP_K):
        pltpu.make_async_copy(ys_ref.at[pl.ds(0, TOK_TILE * ROW_TILE), :], buf.at[cur, k], sems.at[cur]).wait()

    @pl.when(i + 1 < n_tiles)
    def _():
        def both(g, c):
            issue_group(dnext_ref, 1 - cur, g)
            finish_group(g)
            return c

        lax.fori_loop(0, n_groups, both, 0)

    @pl.when(i + 1 >= n_tiles)
    def _():
        def last(g, c):
            finish_group(g)
            return c

        lax.fori_loop(0, n_groups, last, 0)


def _combine(dest_tiles, ys, x1, gates_t, gf, tile0):
    n = x1.shape[0]
    assert n % TOK_TILE == 0
    n_tiles = n // TOK_TILE
    dest_spec = lambda ahead: pl.BlockSpec(
        (1, 1, TOP_K * TOK_TILE), lambda i: (tile0 + jnp.minimum(i + ahead, n_tiles - 1), 0, 0), memory_space=pltpu.SMEM)
    return pl.pallas_call(
        functools.partial(_combine_body, n_tiles=n_tiles),
        grid=(n_tiles,),
        in_specs=[
            dest_spec(1), dest_spec(0),
            pl.BlockSpec(memory_space=pl.ANY),
            pl.BlockSpec((TOK_TILE, D_MODEL), lambda i: (i, 0)),
            pl.BlockSpec((TOK_TILE, TOP_K), lambda i: (tile0 + i, 0)),
            pl.BlockSpec((1, D_MODEL), lambda i: (0, 0)),
        ],
        out_specs=pl.BlockSpec((TOK_TILE, D_MODEL), lambda i: (i, 0)),
        out_shape=jax.ShapeDtypeStruct((n, D_MODEL), F32),
        scratch_shapes=[pltpu.VMEM((2, TOP_K, TOK_TILE * ROW_TILE, LANES), F32), pltpu.SemaphoreType.DMA((2,))],
        compiler_params=pltpu.CompilerParams(dimension_semantics=("arbitrary",), vmem_limit_bytes=VMEM_LIMIT_BYTES),
        name="combine",
    )(dest_tiles, dest_tiles, ys, x1, gates_t, gf)


def _pack_gate_weights(w_rg, w_ig):
    def bd(w):
        w = w.reshape(D_RNN // MXU_DIM, GATE_PACK, LRU_BLOCK, LRU_BLOCK)
        eye = jnp.eye(GATE_PACK, dtype=w.dtype)
        return jnp.einsum('jpik,pq->jpiqk', w, eye).reshape(D_RNN // MXU_DIM, MXU_DIM, MXU_DIM)
    return jnp.concatenate([bd(w_rg), bd(w_ig)], axis=-1).astype(BF16)


def kernel(x_prompt, x_sample, state_conv, state_h, norm1_g, w_in, conv_w, conv_b, w_rg, b_rg, w_ig, b_ig, lru_lambda, ln_v_g, ln_v_b, w_s, b_s, w_proj_a, w_proj_b, w_out, norm2_g, w_router, b_router, w_gate_up, b_gate_up, w_down, b_down, final_norm_g):
    assert norm1_g.shape[0] == 1, "single layer"
    B, S, _ = x_prompt.shape
    Bs, Ss, _ = x_sample.shape
    row = lambda p: p.reshape(1, -1).astype(F32)
    wts = (
        row(norm1_g[0]), w_in[0].astype(BF16), conv_w[0], row(conv_b[0]), _pack_gate_weights(w_rg[0], w_ig[0]),
        row(b_rg[0]), row(b_ig[0]), row(lru_lambda[0]), row(ln_v_g[0]), row(ln_v_b[0]), w_s[0], b_s[0].T,
        w_proj_a[0].astype(BF16), w_proj_b[0].astype(BF16), w_out[0].astype(BF16), row(norm2_g[0]),
        w_router[0].T.astype(BF16), b_router[0].reshape(N_EXPERTS, 1),
    )
    x1_p, xn2_p, lgt_p, convl_p, hl_p = _mixer(
        x_prompt, jnp.zeros((B, SUBLANES, D_RNN), F32), jnp.zeros((B, 1, D_RNN), F32), wts,
        nseq=1, seg=min(MIX_ROWS, S), start_zero=True, want_v=False)
    conv0_s = jnp.pad(state_conv[0], ((0, 0), (SUBLANES - (CONV_WIDTH - 1), 0), (0, 0)))
    x1_s, xn2_s, lgt_s, convl_s, hl_s, v_s = _mixer(
        x_sample, conv0_s, state_h[0][:, None, :], wts,
        nseq=Bs, seg=Ss, start_zero=False, want_v=True)

    n_p, n_s = B * S, Bs * Ss
    n_tok = n_p + n_s
    n_blk = -(-(n_tok * TOP_K + N_EXPERTS * (ROW_BLOCK - 1)) // ROW_BLOCK)
    _, gate, dest, blk_exp, cnt, pstart, nused = _route((lgt_p, lgt_s), n_blk)
    assert n_tok % TOK_TILE == 0
    n_tiles = n_tok // TOK_TILE
    dest_tiles = dest.reshape(TOP_K, n_tiles, TOK_TILE).transpose(1, 0, 2).reshape(n_tiles, 1, TOP_K * TOK_TILE)
    gates_t = gate.T
    cnt, pstart, nused, blk_exp = cnt.reshape(-1), pstart.reshape(-1), nused.reshape(-1), blk_exp.reshape(-1)

    xs = _dispatch(cnt, pstart, nused, dest_tiles, xn2_p, xn2_s, n_blk)
    run_end = (pstart + cnt + (ROW_BLOCK - 1)) // ROW_BLOCK
    ys = _ffn(blk_exp, nused, run_end, xs, w_gate_up[0], b_gate_up[0][:, None, :], w_down[0], b_down[0][:, None, :])
    gf = row(final_norm_g)
    y_p = _combine(dest_tiles, ys, x1_p.reshape(n_p, D_MODEL), gates_t, gf, 0).reshape(B, S, D_MODEL)
    y_s = _combine(dest_tiles, ys, x1_s.reshape(n_s, D_MODEL), gates_t, gf, n_p // TOK_TILE).reshape(Bs, Ss, D_MODEL)

    keep = slice(SUBLANES - (CONV_WIDTH - 1), SUBLANES)
    return (y_p, y_s, convl_p[None, :, keep, :], hl_p.reshape(1, B, D_RNN), convl_s[None, :, keep, :],
            hl_s.reshape(1, Bs, D_RNN), v_s[None])
```

```python
import functools

import jax
import jax.numpy as jnp
from jax import lax
from jax.experimental import pallas as pl
from jax.experimental.pallas import tpu as pltpu

F32 = jnp.float32
BF16 = jnp.bfloat16
I32 = jnp.int32

D_MODEL = 1024
D_RNN = D_MODEL
N_LRU_BLOCKS = 16
LRU_BLOCK = D_RNN // N_LRU_BLOCKS
CONV_WIDTH = 4
LRU_C = 8.0
MLP_CHUNK = 128
N_MLP_GROUPS = 4
MLP_GROUP_DIM = D_MODEL // 8
D_MLP = N_MLP_GROUPS * MLP_GROUP_DIM
N_IN = 2 * D_RNN + 2 * D_MLP + 2 * D_MODEL
N_EXPERTS = 32
TOP_K = 4
D_EXPERT = D_MODEL
SWIGLU_LIMIT = 7.0
SWIGLU_ALPHA = 1.702
RMS_EPS = 1e-6
LN_EPS = 1e-5

SUBLANES = 8
LANES = 128
MXU_DIM = 256
ROW_TILE = D_MODEL // LANES
VMEM_LIMIT_BYTES = 56 * 1024 * 1024

MIX_ROWS = 256
PROJ_CUTS = (512, 3584, 4608)
ROUTE_CHUNK = 256
ROW_BLOCK = 512
TOK_TILE = 256
ISSUE_UNROLL = 8
COMBINE_GROUP = 128
GATE_PACK = MXU_DIM // LRU_BLOCK


def _const_spec(shape):
    nd = len(shape)
    return pl.BlockSpec(shape, lambda *_: (0,) * nd, pipeline_mode=pl.Buffered(1))


def _store_row_tiles(ref, val, n):
    for j in range(ROW_TILE):
        ref[pl.ds(j, n, stride=ROW_TILE), :] = val[:, j * LANES:(j + 1) * LANES]


def _load_row_tiles(ref, n):
    return jnp.concatenate([ref[pl.ds(j, n, stride=ROW_TILE), :] for j in range(ROW_TILE)], axis=1)


def _row_tile(ref, row):
    start = row * ROW_TILE
    if not isinstance(start, int):
        start = pl.multiple_of(start, ROW_TILE)
    return ref.at[pl.ds(start, ROW_TILE), :]


def _tile_copy(src, src_row, dst, dst_row, sem):
    return pltpu.make_async_copy(_row_tile(src, src_row), _row_tile(dst, dst_row), sem)


def _norm1(x, g1_ref):
    ms = jnp.mean(x * x, axis=-1, keepdims=True)
    return (x * lax.rsqrt(ms + RMS_EPS) * g1_ref[...]).astype(BF16)


def _project(x, g1_ref, win_ref):
    return jnp.dot(_norm1(x, g1_ref), win_ref[...], preferred_element_type=F32)


def _restart_state(conv0_ref, h0_ref, state):
    xa_buf, h_tiles, _, _ = state
    nseq = xa_buf.shape[0]
    xa_buf[:, 0:SUBLANES, :] = conv0_ref[...]
    _store_row_tiles(h_tiles, h0_ref[...].reshape(nseq, D_RNN), nseq)


def _mix_tile(x, z, s, wrefs, orefs, state, *, nseq, seg, mc, start_zero, want_v, fill_mxu=lambda slot: None):
    (convw_ref, convb_ref, wg_ref, brg_ref, big_ref, lam_ref, lng_ref, lnb_ref, ws_ref, bst_ref, wpa_ref, wpb_ref,
     wout_ref, g2_ref, wrt_ref, brt_ref) = wrefs
    if want_v:
        x1_ref, xn2_ref, lgt_ref, convl_ref, hl_ref, v_ref = orefs
    else:
        x1_ref, xn2_ref, lgt_ref, convl_ref, hl_ref = orefs
        v_ref = None
    xa_buf, h_tiles, scan_a, scan_b = state
    rows = nseq * seg
    xa = z[:, 0:D_RNN]
    ga = z[:, D_RNN:2 * D_RNN]
    zu = z[:, 2 * D_RNN:2 * D_RNN + D_MLP]
    zv = z[:, 2 * D_RNN + D_MLP:2 * D_RNN + 2 * D_MLP]
    gate_a = z[:, 2 * D_RNN + 2 * D_MLP:2 * D_RNN + 2 * D_MLP + D_MODEL]
    gate_b = z[:, 2 * D_RNN + 2 * D_MLP + D_MODEL:N_IN]

    cw = convw_ref[...]
    xc_parts = []
    for q in range(nseq):
        xa_buf[q, SUBLANES:SUBLANES + seg, :] = xa[q * seg:(q + 1) * seg]
        acc = convb_ref[...] + xa[q * seg:(q + 1) * seg] * cw[CONV_WIDTH - 1:CONV_WIDTH]
        for k in range(CONV_WIDTH - 1):
            off = SUBLANES - (CONV_WIDTH - 1) + k
            acc = acc + xa_buf[q, off:off + seg, :] * cw[k:k + 1]
        xc_parts.append(acc)
        xa_buf[q, 0:SUBLANES, :] = xa_buf[q, seg:seg + SUBLANES, :]
    xc = xc_parts[0] if nseq == 1 else jnp.concatenate(xc_parts, axis=0)
    convl_ref[...] = xa_buf[:, 0:SUBLANES, :]

    u = jax.nn.gelu(zu)
    gv = jax.nn.gelu(zv)
    mu = jnp.mean(gv, axis=-1, keepdims=True)
    vc = gv - mu
    var = jnp.mean(vc * vc, axis=-1, keepdims=True)
    v = vc * lax.rsqrt(var + LN_EPS) * lng_ref[...] + lnb_ref[...]
    if want_v:
        v_ref[...] = v.reshape(nseq, seg, D_MLP)
    vb = v.astype(BF16)

    xcb = xc.astype(BF16)
    r_parts, i_parts = [], []
    for j in range(D_RNN // MXU_DIM):
        gj = jnp.dot(xcb[:, j * MXU_DIM:(j + 1) * MXU_DIM], wg_ref[j], preferred_element_type=F32)
        r_parts.append(gj[:, 0:MXU_DIM])
        i_parts.append(gj[:, MXU_DIM:2 * MXU_DIM])
    fill_mxu(0)

    tri = lax.broadcasted_iota(I32, (mc, mc), 0) >= lax.broadcasted_iota(I32, (mc, mc), 1)
    ws_tri = [jnp.where(tri, ws_ref[g][0:mc, 0:mc], 0.0).astype(BF16) for g in range(N_MLP_GROUPS)]
    bst = bst_ref[...]
    mix_rows = []
    for c in range(rows // mc):
        cols = []
        for g in range(N_MLP_GROUPS):
            vg = vb[c * mc:(c + 1) * mc, g * MLP_GROUP_DIM:(g + 1) * MLP_GROUP_DIM]
            mg = jnp.dot(ws_tri[g], vg, preferred_element_type=F32) + bst[0:mc, g:g + 1]
            cols.append(mg)
        mix_rows.append(jnp.concatenate(cols, axis=1))
    vmix = mix_rows[0] if len(mix_rows) == 1 else jnp.concatenate(mix_rows, axis=0)

    r = jax.nn.sigmoid(jnp.concatenate(r_parts, axis=1) + brg_ref[...])
    ig = jax.nn.sigmoid(jnp.concatenate(i_parts, axis=1) + big_ref[...])
    lam = lam_ref[...]
    softplus_neg = jnp.maximum(-lam, 0.0) + jnp.log1p(jnp.exp(-jnp.abs(lam)))
    log_a = (-LRU_C) * r * softplus_neg
    a = jnp.exp(log_a)
    mult = jnp.sqrt(jnp.maximum(1.0 - a * a, 0.0))
    if start_zero:
        row = lax.broadcasted_iota(I32, (rows, 1), 0)
        mult = jnp.where(jnp.logical_and(row == 0, s == 0), 1.0, mult)
    bterm = mult * ig * xc

    _store_row_tiles(scan_a, a, rows)
    _store_row_tiles(scan_b, bterm, rows)
    for q in range(nseq):
        hq = h_tiles[q * ROW_TILE:(q + 1) * ROW_TILE, :]
        for t in range(seg):
            r = (q * seg + t) * ROW_TILE
            hq = scan_a[r:r + ROW_TILE, :] * hq + scan_b[r:r + ROW_TILE, :]
            scan_b[r:r + ROW_TILE, :] = hq
        h_tiles[q * ROW_TILE:(q + 1) * ROW_TILE, :] = hq
    h = _load_row_tiles(scan_b, rows)
    hl_ref[...] = _load_row_tiles(h_tiles, nseq).reshape(nseq, 1, D_RNN)

    ya = jnp.dot((h * jax.nn.gelu(ga)).astype(BF16), wpa_ref[...], preferred_element_type=F32)
    yb = jnp.dot((u * vmix).astype(BF16), wpb_ref[...], preferred_element_type=F32)
    fill_mxu(1)

    merged = jax.nn.sigmoid(gate_a) * ya + jax.nn.sigmoid(gate_b) * yb
    x1 = x + jnp.dot(merged.astype(BF16), wout_ref[...], preferred_element_type=F32)
    fill_mxu(2)
    x1_ref[...] = x1.reshape(nseq, seg, D_MODEL)

    ms2 = jnp.mean(x1 * x1, axis=-1, keepdims=True)
    xn2 = x1 * lax.rsqrt(ms2 + RMS_EPS) * g2_ref[...]
    _store_row_tiles(xn2_ref, xn2, rows)
    lgt = lax.dot_general(wrt_ref[...], xn2.astype(BF16), (((1,), (1,)), ((), ())), preferred_element_type=F32)
    lgt_ref[...] = lgt + brt_ref[...]


N_MIX_WEIGHTS = 18


def _mixer_whole_body(x_ref, conv0_ref, h0_ref, g1_ref, win_ref, *rest, n_out, **static):
    wrefs = rest[:N_MIX_WEIGHTS - 2]
    orefs = rest[N_MIX_WEIGHTS - 2:N_MIX_WEIGHTS - 2 + n_out]
    state = rest[N_MIX_WEIGHTS - 2 + n_out:]
    x = x_ref[...].reshape(static["nseq"] * static["seg"], D_MODEL)
    _restart_state(conv0_ref, h0_ref, state)
    _mix_tile(x, _project(x, g1_ref, win_ref), 0, wrefs, orefs, state, **static)


def _mixer_pipe_body(xnext_ref, x_ref, conv0_ref, h0_ref, g1_ref, win_ref, *rest, n_out, n_s, **static):
    wrefs = rest[:N_MIX_WEIGHTS - 2]
    orefs = rest[N_MIX_WEIGHTS - 2:N_MIX_WEIGHTS - 2 + n_out]
    *state, z_even, z_odd = rest[N_MIX_WEIGHTS - 2 + n_out:]
    t = pl.program_id(0)
    rows = static["seg"]
    s_prev = lax.rem(jnp.maximum(t - 1, 0), n_s)

    @pl.when(t == 0)
    def _():
        z_odd[...] = jnp.zeros_like(z_odd)

    @pl.when(s_prev == 0)
    def _():
        _restart_state(conv0_ref, h0_ref, state)

    def step(z_new, z_done):
        xn = _norm1(xnext_ref[...].reshape(rows, D_MODEL), g1_ref)

        def project(lo, hi):
            z_new[:, lo:hi] = jnp.dot(xn, win_ref[:, lo:hi], preferred_element_type=F32)

        cuts = (0,) + PROJ_CUTS + (N_IN,)
        project(cuts[0], cuts[1])
        _mix_tile(x_ref[...].reshape(rows, D_MODEL), z_done, s_prev, wrefs, orefs, state,
                  fill_mxu=lambda slot: project(cuts[slot + 1], cuts[slot + 2]), **static)

    @pl.when(lax.rem(t, 2) == 0)
    def _():
        step(z_even, z_odd)

    @pl.when(lax.rem(t, 2) == 1)
    def _():
        step(z_odd, z_even)


def _mixer(x, conv0, h0, wts, *, nseq, seg, start_zero, want_v):
    B, S, _ = x.shape
    assert len(wts) == N_MIX_WEIGHTS
    assert B % nseq == 0 and S % seg == 0 and seg % SUBLANES == 0
    assert nseq == 1 or (seg == S and nseq == B)
    mc = min(seg, MLP_CHUNK)
    assert seg % mc == 0
    rows = nseq * seg
    n_s = S // seg
    n_tiles = (B // nseq) * n_s
    pipelined = nseq == 1
    done = (lambda t: jnp.maximum(t - 1, 0)) if pipelined else (lambda t: t)
    seq_spec = lambda w, tile: pl.BlockSpec((nseq, seg, w), lambda t: (tile(t) // n_s, tile(t) % n_s, 0))
    state_spec = lambda r: pl.BlockSpec((nseq, r, D_RNN), lambda t: (done(t) // n_s, 0, 0))
    in_specs = [seq_spec(D_MODEL, done), state_spec(SUBLANES), state_spec(1)] + [_const_spec(w.shape) for w in wts]
    out_shape = [
        jax.ShapeDtypeStruct((B, S, D_MODEL), F32),
        jax.ShapeDtypeStruct((B * S * ROW_TILE, LANES), F32),
        jax.ShapeDtypeStruct((N_EXPERTS, B * S), F32),
        jax.ShapeDtypeStruct((B, SUBLANES, D_RNN), F32),
        jax.ShapeDtypeStruct((B, 1, D_RNN), F32),
    ]
    out_specs = [
        seq_spec(D_MODEL, done), pl.BlockSpec((rows * ROW_TILE, LANES), lambda t: (done(t), 0)),
        pl.BlockSpec((N_EXPERTS, rows), lambda t: (0, done(t))),
        state_spec(SUBLANES), state_spec(1),
    ]
    if want_v:
        out_shape.append(jax.ShapeDtypeStruct((B, S, D_MLP), F32))
        out_specs.append(seq_spec(D_MLP, done))
    static = dict(n_out=len(out_shape), nseq=nseq, seg=seg, mc=mc, start_zero=start_zero, want_v=want_v)
    scratch = [pltpu.VMEM((nseq, seg + SUBLANES, D_RNN), F32), pltpu.VMEM((nseq * ROW_TILE, LANES), F32),
               pltpu.VMEM((rows * ROW_TILE, LANES), F32), pltpu.VMEM((rows * ROW_TILE, LANES), F32)]
    if pipelined:
        body = functools.partial(_mixer_pipe_body, n_s=n_s, **static)
        grid = (n_tiles + 1,)
        in_specs = [seq_spec(D_MODEL, lambda t: jnp.minimum(t, n_tiles - 1))] + in_specs
        scratch += [pltpu.VMEM((rows, N_IN), F32), pltpu.VMEM((rows, N_IN), F32)]
        operands = (x, x, conv0, h0, *wts)
    else:
        body = functools.partial(_mixer_whole_body, **static)
        grid = (n_tiles,)
        operands = (x, conv0, h0, *wts)
    return pl.pallas_call(
        body,
        grid=grid,
        in_specs=in_specs,
        out_specs=out_specs,
        out_shape=out_shape,
        scratch_shapes=scratch,
        compiler_params=pltpu.CompilerParams(dimension_semantics=("arbitrary",), vmem_limit_bytes=VMEM_LIMIT_BYTES),
        name="mixer",
    )(*operands)


def _route_body(*refs, n_toks, n_blk_pad):
    lgt_refs = refs[:len(n_toks)]
    e_ref, gate_ref, dest_ref, blk_ref, cnt_ref, pstart_ref, nused_ref = refs[len(n_toks):]
    ch = ROUTE_CHUNK
    n_chunks = sum(n_toks) // ch
    eiota = lax.broadcasted_iota(I32, (N_EXPERTS, ch), 0)
    upper = (lax.broadcasted_iota(I32, (ch, ch), 0) < lax.broadcasted_iota(I32, (ch, ch), 1)).astype(BF16)

    def rank_chunk(lgt_ref, first_chunk, c, counts):
        off = pl.multiple_of((first_chunk + c) * ch, ch)
        work = lgt_ref[:, pl.ds(pl.multiple_of(c * ch, ch), ch)]
        sel = jnp.zeros((N_EXPERTS, ch), F32)
        es, vs = [], []
        for _ in range(TOP_K):
            m = jnp.max(work, axis=0, keepdims=True)
            idx = jnp.min(jnp.where(work == m, eiota, N_EXPERTS), axis=0, keepdims=True)
            hit = eiota == idx
            es.append(idx)
            vs.append(m)
            work = jnp.where(hit, -jnp.inf, work)
            sel = jnp.where(hit, 1.0, sel)
        exps = [jnp.exp(v - vs[0]) for v in vs]
        inv = 1.0 / (exps[0] + exps[1] + exps[2] + exps[3])
        before = jnp.dot(sel.astype(BF16), upper, preferred_element_type=F32) + counts
        for k in range(TOP_K):
            rank = jnp.sum(jnp.where(eiota == es[k], before, 0.0), axis=0, keepdims=True)
            e_ref[k:k + 1, pl.ds(off, ch)] = es[k]
            gate_ref[k:k + 1, pl.ds(off, ch)] = exps[k] * inv
            dest_ref[k:k + 1, pl.ds(off, ch)] = rank.astype(I32)
        return counts + jnp.sum(sel, axis=1, keepdims=True)

    counts = jnp.zeros((N_EXPERTS, 1), F32)
    first_chunk = 0
    for lgt_ref, n in zip(lgt_refs, n_toks):
        counts = lax.fori_loop(0, n // ch, functools.partial(rank_chunk, lgt_ref, first_chunk), counts)
        first_chunk += n // ch
    counts = counts.astype(I32)
    shift = ROW_BLOCK.bit_length() - 1
    padded = lax.shift_left(lax.shift_right_logical(counts + (ROW_BLOCK - 1), shift), shift)
    col = lax.broadcasted_iota(I32, (N_EXPERTS, 1), 0)
    pad_end = jnp.zeros((N_EXPERTS, 1), I32)
    for e in range(N_EXPERTS):
        tot = jnp.sum(jnp.where(col <= e, padded, 0), axis=0, keepdims=True)
        pad_end = jnp.where(col == e, tot, pad_end)
    pad_start = pad_end - padded
    cnt_ref[...] = counts
    pstart_ref[...] = pad_start
    nused_ref[...] = lax.shift_right_logical(jnp.max(pad_end, axis=0, keepdims=True), shift)
    blk_row = lax.broadcasted_iota(I32, (N_EXPERTS, n_blk_pad), 1) * ROW_BLOCK
    blk_ref[...] = jnp.minimum(jnp.sum((pad_end <= blk_row).astype(I32), axis=0, keepdims=True), N_EXPERTS - 1)

    def place_chunk(c, carry):
        off = pl.multiple_of(c * ch, ch)
        for k in range(TOP_K):
            ek = e_ref[k:k + 1, pl.ds(off, ch)]
            base = jnp.sum(jnp.where(eiota == ek, pad_start, 0), axis=0, keepdims=True)
            dest_ref[k:k + 1, pl.ds(off, ch)] = dest_ref[k:k + 1, pl.ds(off, ch)] + base
        return carry

    lax.fori_loop(0, n_chunks, place_chunk, 0)


def _route(lgts, n_blk):
    n_toks = tuple(lgt.shape[1] for lgt in lgts)
    assert all(n % ROUTE_CHUNK == 0 for n in n_toks)
    n_tok = sum(n_toks)
    n_blk_pad = -(-n_blk // LANES) * LANES
    full = lambda shape: pl.BlockSpec(shape, lambda i: (0,) * len(shape))
    return pl.pallas_call(
        functools.partial(_route_body, n_toks=n_toks, n_blk_pad=n_blk_pad),
        grid=(1,),
        in_specs=[full(lgt.shape) for lgt in lgts],
        out_specs=[full((TOP_K, n_tok)), full((TOP_K, n_tok)), full((TOP_K, n_tok)), full((1, n_blk_pad)),
                   full((N_EXPERTS, 1)), full((N_EXPERTS, 1)), full((1, 1))],
        out_shape=[jax.ShapeDtypeStruct((TOP_K, n_tok), I32),
                   jax.ShapeDtypeStruct((TOP_K, n_tok), F32),
                   jax.ShapeDtypeStruct((TOP_K, n_tok), I32),
                   jax.ShapeDtypeStruct((1, n_blk_pad), I32),
                   jax.ShapeDtypeStruct((N_EXPERTS, 1), I32),
                   jax.ShapeDtypeStruct((N_EXPERTS, 1), I32),
                   jax.ShapeDtypeStruct((1, 1), I32)],
        compiler_params=pltpu.CompilerParams(dimension_semantics=("arbitrary",), vmem_limit_bytes=VMEM_LIMIT_BYTES),
        name="route",
    )(*lgts)


def _dispatch_body(cnt_ref, pstart_ref, nused_ref, dest_ref, xp_ref, xs_ref, out_ref, zbuf, sem, *,
                   n_prompt_tiles, n_blk):
    i = pl.program_id(0)

    @pl.when(i == 0)
    def _():
        zbuf[...] = jnp.zeros_like(zbuf)
        sizes = [1 << b for b in reversed(range(ROW_BLOCK.bit_length() - 1))]

        def pad_copies(e):
            cnt = cnt_ref[e]
            first = pstart_ref[e] + cnt
            n_pad = (-cnt) & (ROW_BLOCK - 1)
            for size in sizes:
                start = first + (n_pad & ~(2 * size - 1))
                dst = out_ref.at[pl.ds(pl.multiple_of(start * ROW_TILE, ROW_TILE), size * ROW_TILE), :]
                yield (n_pad & size) != 0, pltpu.make_async_copy(zbuf.at[0:size * ROW_TILE, :], dst, sem)

        for e in range(N_EXPERTS):
            for needed, cp in pad_copies(e):
                pl.when(needed)(cp.start)
        for e in range(N_EXPERTS):
            for needed, cp in pad_copies(e):
                pl.when(needed)(cp.wait)

        def blk_copy(b):
            blk_rows = ROW_BLOCK * ROW_TILE
            return pltpu.make_async_copy(
                zbuf, out_ref.at[pl.ds(pl.multiple_of(b * blk_rows, blk_rows), blk_rows), :], sem)

        def fill_blk(b, c):
            blk_copy(b).start()
            return c

        def drain_blk(b, c):
            blk_copy(b).wait()
            return c

        lax.fori_loop(nused_ref[0], n_blk, fill_blk, 0)
        lax.fori_loop(nused_ref[0], n_blk, drain_blk, 0)

    def scatter_rows(src):
        def issue(g, c):
            for u in range(ISSUE_UNROLL):
                r = g * ISSUE_UNROLL + u
                for k in range(TOP_K):
                    _tile_copy(src, r, out_ref, dest_ref[0, 0, k * TOK_TILE + r], sem).start(priority=k % 2)
            return c

        lax.fori_loop(0, TOK_TILE // ISSUE_UNROLL, issue, 0)
        for k in range(TOP_K):
            pltpu.make_async_copy(src, out_ref.at[pl.ds(0, TOK_TILE * ROW_TILE), :], sem).wait()

    @pl.when(i < n_prompt_tiles)
    def _():
        scatter_rows(xp_ref)

    @pl.when(i >= n_prompt_tiles)
    def _():
        scatter_rows(xs_ref)


def _dispatch(cnt, pstart, nused, dest_tiles, xn2_p, xn2_s, n_blk):
    tile_rows = TOK_TILE * ROW_TILE
    n_p = xn2_p.shape[0] // tile_rows
    n_s = xn2_s.shape[0] // tile_rows
    assert xn2_p.shape[0] % tile_rows == 0 and xn2_s.shape[0] % tile_rows == 0
    return pl.pallas_call(
        functools.partial(_dispatch_body, n_prompt_tiles=n_p, n_blk=n_blk),
        grid_spec=pltpu.PrefetchScalarGridSpec(
            num_scalar_prefetch=3,
            grid=(n_p + n_s,),
            in_specs=[
                pl.BlockSpec((1, 1, TOP_K * TOK_TILE), lambda i, *_: (i, 0, 0), memory_space=pltpu.SMEM),
                pl.BlockSpec((tile_rows, LANES), lambda i, *_: (jnp.minimum(i, n_p - 1), 0)),
                pl.BlockSpec((tile_rows, LANES), lambda i, *_: (jnp.maximum(i - n_p, 0), 0)),
            ],
            out_specs=pl.BlockSpec(memory_space=pl.ANY),
            scratch_shapes=[pltpu.VMEM((ROW_BLOCK * ROW_TILE, LANES), F32), pltpu.SemaphoreType.DMA],
        ),
        out_shape=jax.ShapeDtypeStruct((n_blk * ROW_BLOCK * ROW_TILE, LANES), F32),
        compiler_params=pltpu.CompilerParams(dimension_semantics=("arbitrary",), vmem_limit_bytes=VMEM_LIMIT_BYTES),
        name="dispatch",
    )(cnt, pstart, nused, dest_tiles, xn2_p, xn2_s)


def _ffn_body(blk_ref, nused_ref, runend_ref, x_ref, wgu_hbm, bgu_ref, wd_hbm, bd_ref, o_ref, wgu_f32, wd_f32,
              wgu_bf, wd_bf, sems, slot_ref):
    i = pl.program_id(0)
    last = nused_ref[0] - 1
    expert = blk_ref[jnp.minimum(i, last)]
    prev_expert = blk_ref[jnp.minimum(jnp.maximum(i - 1, 0), last)]

    def weight_copies(e, slot):
        return (pltpu.make_async_copy(wgu_hbm.at[e], wgu_f32.at[slot], sems.at[slot]),
                pltpu.make_async_copy(wd_hbm.at[e], wd_f32.at[slot], sems.at[slot]))

    @pl.when(i == 0)
    def _():
        slot_ref[0] = 0
        for cp in weight_copies(expert, 0):
            cp.start()

    @pl.when(jnp.logical_or(i == 0, expert != prev_expert))
    def _():
        slot = slot_ref[0]
        for cp in weight_copies(expert, slot):
            cp.wait()
        next_first = runend_ref[expert]

        @pl.when(next_first <= last)
        def _():
            for cp in weight_copies(blk_ref[jnp.minimum(next_first, last)], 1 - slot):
                cp.start()

        wgu_bf[...] = wgu_f32[slot].astype(BF16)
        wd_bf[...] = wd_f32[slot].astype(BF16)
        slot_ref[0] = 1 - slot

    @pl.when(i < nused_ref[0])
    def _():
        x = _load_row_tiles(x_ref, ROW_BLOCK).astype(BF16)
        gu = jnp.dot(x, wgu_bf[...], preferred_element_type=F32) + bgu_ref[0]
        gate = jnp.minimum(gu[:, 0:D_EXPERT], SWIGLU_LIMIT)
        up = jnp.clip(gu[:, D_EXPERT:2 * D_EXPERT], -SWIGLU_LIMIT, SWIGLU_LIMIT)
        act = (up + 1.0) * (gate * jax.nn.sigmoid(SWIGLU_ALPHA * gate))
        out = jnp.dot(act.astype(BF16), wd_bf[...], preferred_element_type=F32) + bd_ref[0]
        _store_row_tiles(o_ref, out, ROW_BLOCK)

    @pl.when(i >= nused_ref[0])
    def _():
        o_ref[...] = jnp.zeros_like(o_ref)


def _ffn(blk_exp, nused, run_end, xs, wgu, bgu, wd, bd):
    blk_rows = ROW_BLOCK * ROW_TILE
    n_blk = xs.shape[0] // blk_rows
    row_map = lambda i, blk, nu, re: (jnp.minimum(i, nu[0] - 1), 0)
    exp_map = lambda i, blk, nu, re: (blk[jnp.minimum(i, nu[0] - 1)], 0, 0)
    return pl.pallas_call(
        _ffn_body,
        grid_spec=pltpu.PrefetchScalarGridSpec(
            num_scalar_prefetch=3,
            grid=(n_blk,),
            in_specs=[
                pl.BlockSpec((blk_rows, LANES), row_map),
                pl.BlockSpec(memory_space=pl.ANY),
                pl.BlockSpec((1, 1, 2 * D_EXPERT), exp_map),
                pl.BlockSpec(memory_space=pl.ANY),
                pl.BlockSpec((1, 1, D_MODEL), exp_map),
            ],
            out_specs=pl.BlockSpec((blk_rows, LANES), lambda i, blk, nu, re: (i, 0)),
            scratch_shapes=[pltpu.VMEM((2, D_MODEL, 2 * D_EXPERT), F32), pltpu.VMEM((2, D_EXPERT, D_MODEL), F32),
                            pltpu.VMEM((D_MODEL, 2 * D_EXPERT), BF16), pltpu.VMEM((D_EXPERT, D_MODEL), BF16),
                            pltpu.SemaphoreType.DMA((2,)), pltpu.SMEM((1,), I32)],
        ),
        out_shape=jax.ShapeDtypeStruct(xs.shape, F32),
        compiler_params=pltpu.CompilerParams(dimension_semantics=("arbitrary",), vmem_limit_bytes=VMEM_LIMIT_BYTES),
        name="ffn",
    )(blk_exp, nused, run_end, xs, wgu, bgu, wd, bd)


def _combine_body(dnext_ref, dcur_ref, ys_ref, x1_ref, gate_ref, gf_ref, y_ref, buf, sems, *, n_tiles):
    i = pl.program_id(0)
    cur = lax.rem(i, 2)
    n_groups = TOK_TILE // COMBINE_GROUP

    def issue_group(dref, slot, g):
        for u in range(COMBINE_GROUP):
            r = g * COMBINE_GROUP + u
            for k in range(TOP_K):
                _tile_copy(ys_ref, dref[0, 0, k * TOK_TILE + r], buf.at[slot, k], r, sems.at[slot]).start(priority=k % 2)

    def finish_group(g):
        base = pl.multiple_of(g * COMBINE_GROUP, COMBINE_GROUP)
        gates = gate_ref[pl.ds(base, COMBINE_GROUP), :]
        moe = None
        for k in range(TOP_K):
            rows = jnp.concatenate(
                [buf[cur, k, pl.ds(base * ROW_TILE + j, COMBINE_GROUP, stride=ROW_TILE), :] for j in range(ROW_TILE)],
                axis=1)
            term = rows * gates[:, k:k + 1]
            moe = term if moe is None else moe + term
        x2 = x1_ref[pl.ds(base, COMBINE_GROUP), :] + moe
        ms = jnp.mean(x2 * x2, axis=-1, keepdims=True)
        y_ref[pl.ds(base, COMBINE_GROUP), :] = x2 * lax.rsqrt(ms + RMS_EPS) * gf_ref[...]

    @pl.when(i == 0)
    def _():
        def first(g, c):
            issue_group(dcur_ref, 0, g)
            return c

        lax.fori_loop(0, n_groups, first, 0)

    for k in range(TOP_K):
        pltpu.make_async_copy(ys_ref.at[pl.ds(0, TOK_TILE * ROW_TILE), :], buf.at[cur, k], sems.at[cur]).wait()

    @pl.when(i + 1 < n_tiles)
    def _():
        def both(g, c):
            issue_group(dnext_ref, 1 - cur, g)
            finish_group(g)
            return c

        lax.fori_loop(0, n_groups, both, 0)

    @pl.when(i + 1 >= n_tiles)
    def _():
        def last(g, c):
            finish_group(g)
            return c

        lax.fori_loop(0, n_groups, last, 0)


def _combine(dest_tiles, ys, x1, gates_t, gf, tile0):
    n = x1.shape[0]
    assert n % TOK_TILE == 0
    n_tiles = n // TOK_TILE
    dest_spec = lambda ahead: pl.BlockSpec(
        (1, 1, TOP_K * TOK_TILE), lambda i: (tile0 + jnp.minimum(i + ahead, n_tiles - 1), 0, 0), memory_space=pltpu.SMEM)
    return pl.pallas_call(
        functools.partial(_combine_body, n_tiles=n_tiles),
        grid=(n_tiles,),
        in_specs=[
            dest_spec(1), dest_spec(0),
            pl.BlockSpec(memory_space=pl.ANY),
            pl.BlockSpec((TOK_TILE, D_MODEL), lambda i: (i, 0)),
            pl.BlockSpec((TOK_TILE, TOP_K), lambda i: (tile0 + i, 0)),
            pl.BlockSpec((1, D_MODEL), lambda i: (0, 0)),
        ],
        out_specs=pl.BlockSpec((TOK_TILE, D_MODEL), lambda i: (i, 0)),
        out_shape=jax.ShapeDtypeStruct((n, D_MODEL), F32),
        scratch_shapes=[pltpu.VMEM((2, TOP_K, TOK_TILE * ROW_TILE, LANES), F32), pltpu.SemaphoreType.DMA((2,))],
        compiler_params=pltpu.CompilerParams(dimension_semantics=("arbitrary",), vmem_limit_bytes=VMEM_LIMIT_BYTES),
        name="combine",
    )(dest_tiles, dest_tiles, ys, x1, gates_t, gf)


def _pack_gate_weights(w_rg, w_ig):
    def bd(w):
        w = w.reshape(D_RNN // MXU_DIM, GATE_PACK, LRU_BLOCK, LRU_BLOCK)
        eye = jnp.eye(GATE_PACK, dtype=w.dtype)
        return jnp.einsum('jpik,pq->jpiqk', w, eye).reshape(D_RNN // MXU_DIM, MXU_DIM, MXU_DIM)
    return jnp.concatenate([bd(w_rg), bd(w_ig)], axis=-1).astype(BF16)


def kernel(x_prompt, x_sample, state_conv, state_h, norm1_g, w_in, conv_w, conv_b, w_rg, b_rg, w_ig, b_ig, lru_lambda, ln_v_g, ln_v_b, w_s, b_s, w_proj_a, w_proj_b, w_out, norm2_g, w_router, b_router, w_gate_up, b_gate_up, w_down, b_down, final_norm_g):
    assert norm1_g.shape[0] == 1, "single layer"
    B, S, _ = x_prompt.shape
    Bs, Ss, _ = x_sample.shape
    row = lambda p: p.reshape(1, -1).astype(F32)
    wts = (
        row(norm1_g[0]), w_in[0].astype(BF16), conv_w[0], row(conv_b[0]), _pack_gate_weights(w_rg[0], w_ig[0]),
        row(b_rg[0]), row(b_ig[0]), row(lru_lambda[0]), row(ln_v_g[0]), row(ln_v_b[0]), w_s[0], b_s[0].T,
        w_proj_a[0].astype(BF16), w_proj_b[0].astype(BF16), w_out[0].astype(BF16), row(norm2_g[0]),
        w_router[0].T.astype(BF16), b_router[0].reshape(N_EXPERTS, 1),
    )
    x1_p, xn2_p, lgt_p, convl_p, hl_p = _mixer(
        x_prompt, jnp.zeros((B, SUBLANES, D_RNN), F32), jnp.zeros((B, 1, D_RNN), F32), wts,
        nseq=1, seg=min(MIX_ROWS, S), start_zero=True, want_v=False)
    conv0_s = jnp.pad(state_conv[0], ((0, 0), (SUBLANES - (CONV_WIDTH - 1), 0), (0, 0)))
    x1_s, xn2_s, lgt_s, convl_s, hl_s, v_s = _mixer(
        x_sample, conv0_s, state_h[0][:, None, :], wts,
        nseq=Bs, seg=Ss, start_zero=False, want_v=True)

    n_p, n_s = B * S, Bs * Ss
    n_tok = n_p + n_s
    n_blk = -(-(n_tok * TOP_K + N_EXPERTS * (ROW_BLOCK - 1)) // ROW_BLOCK)
    _, gate, dest, blk_exp, cnt, pstart, nused = _route((lgt_p, lgt_s), n_blk)
    assert n_tok % TOK_TILE == 0
    n_tiles = n_tok // TOK_TILE
    dest_tiles = dest.reshape(TOP_K, n_tiles, TOK_TILE).transpose(1, 0, 2).reshape(n_tiles, 1, TOP_K * TOK_TILE)
    gates_t = gate.T
    cnt, pstart, nused, blk_exp = cnt.reshape(-1), pstart.reshape(-1), nused.reshape(-1), blk_exp.reshape(-1)

    xs = _dispatch(cnt, pstart, nused, dest_tiles, xn2_p, xn2_s, n_blk)
    run_end = (pstart + cnt + (ROW_BLOCK - 1)) // ROW_BLOCK
    ys = _ffn(blk_exp, nused, run_end, xs, w_gate_up[0], b_gate_up[0][:, None, :], w_down[0], b_down[0][:, None, :])
    gf = row(final_norm_g)
    y_p = _combine(dest_tiles, ys, x1_p.reshape(n_p, D_MODEL), gates_t, gf, 0).reshape(B, S, D_MODEL)
    y_s = _combine(dest_tiles, ys, x1_s.reshape(n_s, D_MODEL), gates_t, gf, n_p // TOK_TILE).reshape(Bs, Ss, D_MODEL)

    keep = slice(SUBLANES - (CONV_WIDTH - 1), SUBLANES)
    return (y_p, y_s, convl_p[None, :, keep, :], hl_p.reshape(1, B, D_RNN), convl_s[None, :, keep, :],
            hl_s.reshape(1, Bs, D_RNN), v_s[None])
```

```python
import functools

import jax
import jax.numpy as jnp
from jax import lax
from jax.experimental import pallas as pl
from jax.experimental.pallas import tpu as pltpu

F32 = jnp.float32
BF16 = jnp.bfloat16
I32 = jnp.int32

D_MODEL = 1024
D_RNN = D_MODEL
N_LRU_BLOCKS = 16
LRU_BLOCK = D_RNN // N_LRU_BLOCKS
CONV_WIDTH = 4
LRU_C = 8.0
MLP_CHUNK = 128
N_MLP_GROUPS = 4
MLP_GROUP_DIM = D_MODEL // 8
D_MLP = N_MLP_GROUPS * MLP_GROUP_DIM
N_IN = 2 * D_RNN + 2 * D_MLP + 2 * D_MODEL
N_EXPERTS = 32
TOP_K = 4
D_EXPERT = D_MODEL
SWIGLU_LIMIT = 7.0
SWIGLU_ALPHA = 1.702
RMS_EPS = 1e-6
LN_EPS = 1e-5

SUBLANES = 8
LANES = 128
MXU_DIM = 256
ROW_TILE = D_MODEL // LANES
VMEM_LIMIT_BYTES = 56 * 1024 * 1024

MIX_ROWS = 256
PROJ_CUTS = (512, 3584, 4608)
ROUTE_CHUNK = 256
ROW_BLOCK = 512
TOK_TILE = 256
ISSUE_UNROLL = 8
COMBINE_GROUP = 256
GATE_PACK = MXU_DIM // LRU_BLOCK


def _const_spec(shape):
    nd = len(shape)
    return pl.BlockSpec(shape, lambda *_: (0,) * nd, pipeline_mode=pl.Buffered(1))


def _store_row_tiles(ref, val, n):
    for j in range(ROW_TILE):
        ref[pl.ds(j, n, stride=ROW_TILE), :] = val[:, j * LANES:(j + 1) * LANES]


def _load_row_tiles(ref, n):
    return jnp.concatenate([ref[pl.ds(j, n, stride=ROW_TILE), :] for j in range(ROW_TILE)], axis=1)


def _row_tile(ref, row):
    start = row * ROW_TILE
    if not isinstance(start, int):
        start = pl.multiple_of(start, ROW_TILE)
    return ref.at[pl.ds(start, ROW_TILE), :]


def _tile_copy(src, src_row, dst, dst_row, sem):
    return pltpu.make_async_copy(_row_tile(src, src_row), _row_tile(dst, dst_row), sem)


def _norm1(x, g1_ref):
    ms = jnp.mean(x * x, axis=-1, keepdims=True)
    return (x * lax.rsqrt(ms + RMS_EPS) * g1_ref[...]).astype(BF16)


def _project(x, g1_ref, win_ref):
    return jnp.dot(_norm1(x, g1_ref), win_ref[...], preferred_element_type=F32)


def _restart_state(conv0_ref, h0_ref, state):
    xa_buf, h_tiles, _, _ = state
    nseq = xa_buf.shape[0]
    xa_buf[:, 0:SUBLANES, :] = conv0_ref[...]
    _store_row_tiles(h_tiles, h0_ref[...].reshape(nseq, D_RNN), nseq)


def _mix_tile(x, z, s, wrefs, orefs, state, *, nseq, seg, mc, start_zero, want_v, fill_mxu=lambda slot: None):
    (convw_ref, convb_ref, wg_ref, brg_ref, big_ref, lam_ref, lng_ref, lnb_ref, ws_ref, bst_ref, wpa_ref, wpb_ref,
     wout_ref, g2_ref, wrt_ref, brt_ref) = wrefs
    if want_v:
        x1_ref, xn2_ref, lgt_ref, convl_ref, hl_ref, v_ref = orefs
    else:
        x1_ref, xn2_ref, lgt_ref, convl_ref, hl_ref = orefs
        v_ref = None
    xa_buf, h_tiles, scan_a, scan_b = state
    rows = nseq * seg
    xa = z[:, 0:D_RNN]
    ga = z[:, D_RNN:2 * D_RNN]
    zu = z[:, 2 * D_RNN:2 * D_RNN + D_MLP]
    zv = z[:, 2 * D_RNN + D_MLP:2 * D_RNN + 2 * D_MLP]
    gate_a = z[:, 2 * D_RNN + 2 * D_MLP:2 * D_RNN + 2 * D_MLP + D_MODEL]
    gate_b = z[:, 2 * D_RNN + 2 * D_MLP + D_MODEL:N_IN]

    cw = convw_ref[...]
    xc_parts = []
    for q in range(nseq):
        xa_buf[q, SUBLANES:SUBLANES + seg, :] = xa[q * seg:(q + 1) * seg]
        acc = convb_ref[...] + xa[q * seg:(q + 1) * seg] * cw[CONV_WIDTH - 1:CONV_WIDTH]
        for k in range(CONV_WIDTH - 1):
            off = SUBLANES - (CONV_WIDTH - 1) + k
            acc = acc + xa_buf[q, off:off + seg, :] * cw[k:k + 1]
        xc_parts.append(acc)
        xa_buf[q, 0:SUBLANES, :] = xa_buf[q, seg:seg + SUBLANES, :]
    xc = xc_parts[0] if nseq == 1 else jnp.concatenate(xc_parts, axis=0)
    convl_ref[...] = xa_buf[:, 0:SUBLANES, :]

    u = jax.nn.gelu(zu)
    gv = jax.nn.gelu(zv)
    mu = jnp.mean(gv, axis=-1, keepdims=True)
    vc = gv - mu
    var = jnp.mean(vc * vc, axis=-1, keepdims=True)
    v = vc * lax.rsqrt(var + LN_EPS) * lng_ref[...] + lnb_ref[...]
    if want_v:
        v_ref[...] = v.reshape(nseq, seg, D_MLP)
    vb = v.astype(BF16)

    xcb = xc.astype(BF16)
    r_parts, i_parts = [], []
    for j in range(D_RNN // MXU_DIM):
        gj = jnp.dot(xcb[:, j * MXU_DIM:(j + 1) * MXU_DIM], wg_ref[j], preferred_element_type=F32)
        r_parts.append(gj[:, 0:MXU_DIM])
        i_parts.append(gj[:, MXU_DIM:2 * MXU_DIM])
    fill_mxu(0)

    tri = lax.broadcasted_iota(I32, (mc, mc), 0) >= lax.broadcasted_iota(I32, (mc, mc), 1)
    ws_tri = [jnp.where(tri, ws_ref[g][0:mc, 0:mc], 0.0).astype(BF16) for g in range(N_MLP_GROUPS)]
    bst = bst_ref[...]
    mix_rows = []
    for c in range(rows // mc):
        cols = []
        for g in range(N_MLP_GROUPS):
            vg = vb[c * mc:(c + 1) * mc, g * MLP_GROUP_DIM:(g + 1) * MLP_GROUP_DIM]
            mg = jnp.dot(ws_tri[g], vg, preferred_element_type=F32) + bst[0:mc, g:g + 1]
            cols.append(mg)
        mix_rows.append(jnp.concatenate(cols, axis=1))
    vmix = mix_rows[0] if len(mix_rows) == 1 else jnp.concatenate(mix_rows, axis=0)

    r = jax.nn.sigmoid(jnp.concatenate(r_parts, axis=1) + brg_ref[...])
    ig = jax.nn.sigmoid(jnp.concatenate(i_parts, axis=1) + big_ref[...])
    lam = lam_ref[...]
    softplus_neg = jnp.maximum(-lam, 0.0) + jnp.log1p(jnp.exp(-jnp.abs(lam)))
    log_a = (-LRU_C) * r * softplus_neg
    a = jnp.exp(log_a)
    mult = jnp.sqrt(jnp.maximum(1.0 - a * a, 0.0))
    if start_zero:
        row = lax.broadcasted_iota(I32, (rows, 1), 0)
        mult = jnp.where(jnp.logical_and(row == 0, s == 0), 1.0, mult)
    bterm = mult * ig * xc

    _store_row_tiles(scan_a, a, rows)
    _store_row_tiles(scan_b, bterm, rows)
    for q in range(nseq):
        hq = h_tiles[q * ROW_TILE:(q + 1) * ROW_TILE, :]
        for t in range(seg):
            r = (q * seg + t) * ROW_TILE
            hq = scan_a[r:r + ROW_TILE, :] * hq + scan_b[r:r + ROW_TILE, :]
            scan_b[r:r + ROW_TILE, :] = hq
        h_tiles[q * ROW_TILE:(q + 1) * ROW_TILE, :] = hq
    h = _load_row_tiles(scan_b, rows)
    hl_ref[...] = _load_row_tiles(h_tiles, nseq).reshape(nseq, 1, D_RNN)

    ya = jnp.dot((h * jax.nn.gelu(ga)).astype(BF16), wpa_ref[...], preferred_element_type=F32)
    yb = jnp.dot((u * vmix).astype(BF16), wpb_ref[...], preferred_element_type=F32)
    fill_mxu(1)

    merged = jax.nn.sigmoid(gate_a) * ya + jax.nn.sigmoid(gate_b) * yb
    x1 = x + jnp.dot(merged.astype(BF16), wout_ref[...], preferred_element_type=F32)
    fill_mxu(2)
    x1_ref[...] = x1.reshape(nseq, seg, D_MODEL)

    ms2 = jnp.mean(x1 * x1, axis=-1, keepdims=True)
    xn2 = x1 * lax.rsqrt(ms2 + RMS_EPS) * g2_ref[...]
    _store_row_tiles(xn2_ref, xn2, rows)
    lgt = lax.dot_general(wrt_ref[...], xn2.astype(BF16), (((1,), (1,)), ((), ())), preferred_element_type=F32)
    lgt_ref[...] = lgt + brt_ref[...]


N_MIX_WEIGHTS = 18


def _mixer_whole_body(x_ref, conv0_ref, h0_ref, g1_ref, win_ref, *rest, n_out, **static):
    wrefs = rest[:N_MIX_WEIGHTS - 2]
    orefs = rest[N_MIX_WEIGHTS - 2:N_MIX_WEIGHTS - 2 + n_out]
    state = rest[N_MIX_WEIGHTS - 2 + n_out:]
    x = x_ref[...].reshape(static["nseq"] * static["seg"], D_MODEL)
    _restart_state(conv0_ref, h0_ref, state)
    _mix_tile(x, _project(x, g1_ref, win_ref), 0, wrefs, orefs, state, **static)


def _mixer_pipe_body(xnext_ref, x_ref, conv0_ref, h0_ref, g1_ref, win_ref, *rest, n_out, n_s, **static):
    wrefs = rest[:N_MIX_WEIGHTS - 2]
    orefs = rest[N_MIX_WEIGHTS - 2:N_MIX_WEIGHTS - 2 + n_out]
    *state, z_even, z_odd = rest[N_MIX_WEIGHTS - 2 + n_out:]
    t = pl.program_id(0)
    rows = static["seg"]
    s_prev = lax.rem(jnp.maximum(t - 1, 0), n_s)

    @pl.when(t == 0)
    def _():
        z_odd[...] = jnp.zeros_like(z_odd)

    @pl.when(s_prev == 0)
    def _():
        _restart_state(conv0_ref, h0_ref, state)

    def step(z_new, z_done):
        xn = _norm1(xnext_ref[...].reshape(rows, D_MODEL), g1_ref)

        def project(lo, hi):
            z_new[:, lo:hi] = jnp.dot(xn, win_ref[:, lo:hi], preferred_element_type=F32)

        cuts = (0,) + PROJ_CUTS + (N_IN,)
        project(cuts[0], cuts[1])
        _mix_tile(x_ref[...].reshape(rows, D_MODEL), z_done, s_prev, wrefs, orefs, state,
                  fill_mxu=lambda slot: project(cuts[slot + 1], cuts[slot + 2]), **static)

    @pl.when(lax.rem(t, 2) == 0)
    def _():
        step(z_even, z_odd)

    @pl.when(lax.rem(t, 2) == 1)
    def _():
        step(z_odd, z_even)


def _mixer(x, conv0, h0, wts, *, nseq, seg, start_zero, want_v):
    B, S, _ = x.shape
    assert len(wts) == N_MIX_WEIGHTS
    assert B % nseq == 0 and S % seg == 0 and seg % SUBLANES == 0
    assert nseq == 1 or (seg == S and nseq == B)
    mc = min(seg, MLP_CHUNK)
    assert seg % mc == 0
    rows = nseq * seg
    n_s = S // seg
    n_tiles = (B // nseq) * n_s
    pipelined = nseq == 1
    done = (lambda t: jnp.maximum(t - 1, 0)) if pipelined else (lambda t: t)
    seq_spec = lambda w, tile: pl.BlockSpec((nseq, seg, w), lambda t: (tile(t) // n_s, tile(t) % n_s, 0))
    state_spec = lambda r: pl.BlockSpec((nseq, r, D_RNN), lambda t: (done(t) // n_s, 0, 0))
    in_specs = [seq_spec(D_MODEL, done), state_spec(SUBLANES), state_spec(1)] + [_const_spec(w.shape) for w in wts]
    out_shape = [
        jax.ShapeDtypeStruct((B, S, D_MODEL), F32),
        jax.ShapeDtypeStruct((B * S * ROW_TILE, LANES), F32),
        jax.ShapeDtypeStruct((N_EXPERTS, B * S), F32),
        jax.ShapeDtypeStruct((B, SUBLANES, D_RNN), F32),
        jax.ShapeDtypeStruct((B, 1, D_RNN), F32),
    ]
    out_specs = [
        seq_spec(D_MODEL, done), pl.BlockSpec((rows * ROW_TILE, LANES), lambda t: (done(t), 0)),
        pl.BlockSpec((N_EXPERTS, rows), lambda t: (0, done(t))),
        state_spec(SUBLANES), state_spec(1),
    ]
    if want_v:
        out_shape.append(jax.ShapeDtypeStruct((B, S, D_MLP), F32))
        out_specs.append(seq_spec(D_MLP, done))
    static = dict(n_out=len(out_shape), nseq=nseq, seg=seg, mc=mc, start_zero=start_zero, want_v=want_v)
    scratch = [pltpu.VMEM((nseq, seg + SUBLANES, D_RNN), F32), pltpu.VMEM((nseq * ROW_TILE, LANES), F32),
               pltpu.VMEM((rows * ROW_TILE, LANES), F32), pltpu.VMEM((rows * ROW_TILE, LANES), F32)]
    if pipelined:
        body = functools.partial(_mixer_pipe_body, n_s=n_s, **static)
        grid = (n_tiles + 1,)
        in_specs = [seq_spec(D_MODEL, lambda t: jnp.minimum(t, n_tiles - 1))] + in_specs
        scratch += [pltpu.VMEM((rows, N_IN), F32), pltpu.VMEM((rows, N_IN), F32)]
        operands = (x, x, conv0, h0, *wts)
    else:
        body = functools.partial(_mixer_whole_body, **static)
        grid = (n_tiles,)
        operands = (x, conv0, h0, *wts)
    return pl.pallas_call(
        body,
        grid=grid,
        in_specs=in_specs,
        out_specs=out_specs,
        out_shape=out_shape,
        scratch_shapes=scratch,
        compiler_params=pltpu.CompilerParams(dimension_semantics=("arbitrary",), vmem_limit_bytes=VMEM_LIMIT_BYTES),
        name="mixer",
    )(*operands)


def _route_body(*refs, n_toks, n_blk_pad):
    lgt_refs = refs[:len(n_toks)]
    e_ref, gate_ref, dest_ref, blk_ref, cnt_ref, pstart_ref, nused_ref = refs[len(n_toks):]
    ch = ROUTE_CHUNK
    n_chunks = sum(n_toks) // ch
    eiota = lax.broadcasted_iota(I32, (N_EXPERTS, ch), 0)
    upper = (lax.broadcasted_iota(I32, (ch, ch), 0) < lax.broadcasted_iota(I32, (ch, ch), 1)).astype(BF16)

    def rank_chunk(lgt_ref, first_chunk, c, counts):
        off = pl.multiple_of((first_chunk + c) * ch, ch)
        work = lgt_ref[:, pl.ds(pl.multiple_of(c * ch, ch), ch)]
        sel = jnp.zeros((N_EXPERTS, ch), F32)
        es, vs = [], []
        for _ in range(TOP_K):
            m = jnp.max(work, axis=0, keepdims=True)
            idx = jnp.min(jnp.where(work == m, eiota, N_EXPERTS), axis=0, keepdims=True)
            hit = eiota == idx
            es.append(idx)
            vs.append(m)
            work = jnp.where(hit, -jnp.inf, work)
            sel = jnp.where(hit, 1.0, sel)
        exps = [jnp.exp(v - vs[0]) for v in vs]
        inv = 1.0 / (exps[0] + exps[1] + exps[2] + exps[3])
        before = jnp.dot(sel.astype(BF16), upper, preferred_element_type=F32) + counts
        for k in range(TOP_K):
            rank = jnp.sum(jnp.where(eiota == es[k], before, 0.0), axis=0, keepdims=True)
            e_ref[k:k + 1, pl.ds(off, ch)] = es[k]
            gate_ref[k:k + 1, pl.ds(off, ch)] = exps[k] * inv
            dest_ref[k:k + 1, pl.ds(off, ch)] = rank.astype(I32)
        return counts + jnp.sum(sel, axis=1, keepdims=True)

    counts = jnp.zeros((N_EXPERTS, 1), F32)
    first_chunk = 0
    for lgt_ref, n in zip(lgt_refs, n_toks):
        counts = lax.fori_loop(0, n // ch, functools.partial(rank_chunk, lgt_ref, first_chunk), counts)
        first_chunk += n // ch
    counts = counts.astype(I32)
    shift = ROW_BLOCK.bit_length() - 1
    padded = lax.shift_left(lax.shift_right_logical(counts + (ROW_BLOCK - 1), shift), shift)
    col = lax.broadcasted_iota(I32, (N_EXPERTS, 1), 0)
    pad_end = jnp.zeros((N_EXPERTS, 1), I32)
    for e in range(N_EXPERTS):
        tot = jnp.sum(jnp.where(col <= e, padded, 0), axis=0, keepdims=True)
        pad_end = jnp.where(col == e, tot, pad_end)
    pad_start = pad_end - padded
    cnt_ref[...] = counts
    pstart_ref[...] = pad_start
    nused_ref[...] = lax.shift_right_logical(jnp.max(pad_end, axis=0, keepdims=True), shift)
    blk_row = lax.broadcasted_iota(I32, (N_EXPERTS, n_blk_pad), 1) * ROW_BLOCK
    blk_ref[...] = jnp.minimum(jnp.sum((pad_end <= blk_row).astype(I32), axis=0, keepdims=True), N_EXPERTS - 1)

    def place_chunk(c, carry):
        off = pl.multiple_of(c * ch, ch)
        for k in range(TOP_K):
            ek = e_ref[k:k + 1, pl.ds(off, ch)]
            base = jnp.sum(jnp.where(eiota == ek, pad_start, 0), axis=0, keepdims=True)
            dest_ref[k:k + 1, pl.ds(off, ch)] = dest_ref[k:k + 1, pl.ds(off, ch)] + base
        return carry

    lax.fori_loop(0, n_chunks, place_chunk, 0)


def _route(lgts, n_blk):
    n_toks = tuple(lgt.shape[1] for lgt in lgts)
    assert all(n % ROUTE_CHUNK == 0 for n in n_toks)
    n_tok = sum(n_toks)
    n_blk_pad = -(-n_blk // LANES) * LANES
    full = lambda shape: pl.BlockSpec(shape, lambda i: (0,) * len(shape))
    return pl.pallas_call(
        functools.partial(_route_body, n_toks=n_toks, n_blk_pad=n_blk_pad),
        grid=(1,),
        in_specs=[full(lgt.shape) for lgt in lgts],
        out_specs=[full((TOP_K, n_tok)), full((TOP_K, n_tok)), full((TOP_K, n_tok)), full((1, n_blk_pad)),
                   full((N_EXPERTS, 1)), full((N_EXPERTS, 1)), full((1, 1))],
        out_shape=[jax.ShapeDtypeStruct((TOP_K, n_tok), I32),
                   jax.ShapeDtypeStruct((TOP_K, n_tok), F32),
                   jax.ShapeDtypeStruct((TOP_K, n_tok), I32),
                   jax.ShapeDtypeStruct((1, n_blk_pad), I32),
                   jax.ShapeDtypeStruct((N_EXPERTS, 1), I32),
                   jax.ShapeDtypeStruct((N_EXPERTS, 1), I32),
                   jax.ShapeDtypeStruct((1, 1), I32)],
        compiler_params=pltpu.CompilerParams(dimension_semantics=("arbitrary",), vmem_limit_bytes=VMEM_LIMIT_BYTES),
        name="route",
    )(*lgts)


def _dispatch_body(cnt_ref, pstart_ref, nused_ref, dest_ref, xp_ref, xs_ref, out_ref, zbuf, sem, *,
                   n_prompt_tiles, n_blk):
    i = pl.program_id(0)

    @pl.when(i == 0)
    def _():
        zbuf[...] = jnp.zeros_like(zbuf)
        sizes = [1 << b for b in reversed(range(ROW_BLOCK.bit_length() - 1))]

        def pad_copies(e):
            cnt = cnt_ref[e]
            first = pstart_ref[e] + cnt
            n_pad = (-cnt) & (ROW_BLOCK - 1)
            for size in sizes:
                start = first + (n_pad & ~(2 * size - 1))
                dst = out_ref.at[pl.ds(pl.multiple_of(start * ROW_TILE, ROW_TILE), size * ROW_TILE), :]
                yield (n_pad & size) != 0, pltpu.make_async_copy(zbuf.at[0:size * ROW_TILE, :], dst, sem)

        for e in range(N_EXPERTS):
            for needed, cp in pad_copies(e):
                pl.when(needed)(cp.start)
        for e in range(N_EXPERTS):
            for needed, cp in pad_copies(e):
                pl.when(needed)(cp.wait)

        def blk_copy(b):
            blk_rows = ROW_BLOCK * ROW_TILE
            return pltpu.make_async_copy(
                zbuf, out_ref.at[pl.ds(pl.multiple_of(b * blk_rows, blk_rows), blk_rows), :], sem)

        def fill_blk(b, c):
            blk_copy(b).start()
            return c

        def drain_blk(b, c):
            blk_copy(b).wait()
            return c

        lax.fori_loop(nused_ref[0], n_blk, fill_blk, 0)
        lax.fori_loop(nused_ref[0], n_blk, drain_blk, 0)

    def scatter_rows(src):
        def issue(g, c):
            for u in range(ISSUE_UNROLL):
                r = g * ISSUE_UNROLL + u
                for k in range(TOP_K):
                    _tile_copy(src, r, out_ref, dest_ref[0, 0, k * TOK_TILE + r], sem).start(priority=k % 2)
            return c

        lax.fori_loop(0, TOK_TILE // ISSUE_UNROLL, issue, 0)
        for k in range(TOP_K):
            pltpu.make_async_copy(src, out_ref.at[pl.ds(0, TOK_TILE * ROW_TILE), :], sem).wait()

    @pl.when(i < n_prompt_tiles)
    def _():
        scatter_rows(xp_ref)

    @pl.when(i >= n_prompt_tiles)
    def _():
        scatter_rows(xs_ref)


def _dispatch(cnt, pstart, nused, dest_tiles, xn2_p, xn2_s, n_blk):
    tile_rows = TOK_TILE * ROW_TILE
    n_p = xn2_p.shape[0] // tile_rows
    n_s = xn2_s.shape[0] // tile_rows
    assert xn2_p.shape[0] % tile_rows == 0 and xn2_s.shape[0] % tile_rows == 0
    return pl.pallas_call(
        functools.partial(_dispatch_body, n_prompt_tiles=n_p, n_blk=n_blk),
        grid_spec=pltpu.PrefetchScalarGridSpec(
            num_scalar_prefetch=3,
            grid=(n_p + n_s,),
            in_specs=[
                pl.BlockSpec((1, 1, TOP_K * TOK_TILE), lambda i, *_: (i, 0, 0), memory_space=pltpu.SMEM),
                pl.BlockSpec((tile_rows, LANES), lambda i, *_: (jnp.minimum(i, n_p - 1), 0)),
                pl.BlockSpec((tile_rows, LANES), lambda i, *_: (jnp.maximum(i - n_p, 0), 0)),
            ],
            out_specs=pl.BlockSpec(memory_space=pl.ANY),
            scratch_shapes=[pltpu.VMEM((ROW_BLOCK * ROW_TILE, LANES), F32), pltpu.SemaphoreType.DMA],
        ),
        out_shape=jax.ShapeDtypeStruct((n_blk * ROW_BLOCK * ROW_TILE, LANES), F32),
        compiler_params=pltpu.CompilerParams(dimension_semantics=("arbitrary",), vmem_limit_bytes=VMEM_LIMIT_BYTES),
        name="dispatch",
    )(cnt, pstart, nused, dest_tiles, xn2_p, xn2_s)


def _ffn_body(blk_ref, nused_ref, runend_ref, x_ref, wgu_hbm, bgu_ref, wd_hbm, bd_ref, o_ref, wgu_f32, wd_f32,
              wgu_bf, wd_bf, sems, slot_ref):
    i = pl.program_id(0)
    last = nused_ref[0] - 1
    expert = blk_ref[jnp.minimum(i, last)]
    prev_expert = blk_ref[jnp.minimum(jnp.maximum(i - 1, 0), last)]

    def weight_copies(e, slot):
        return (pltpu.make_async_copy(wgu_hbm.at[e], wgu_f32.at[slot], sems.at[slot]),
                pltpu.make_async_copy(wd_hbm.at[e], wd_f32.at[slot], sems.at[slot]))

    @pl.when(i == 0)
    def _():
        slot_ref[0] = 0
        for cp in weight_copies(expert, 0):
            cp.start()

    @pl.when(jnp.logical_or(i == 0, expert != prev_expert))
    def _():
        slot = slot_ref[0]
        for cp in weight_copies(expert, slot):
            cp.wait()
        next_first = runend_ref[expert]

        @pl.when(next_first <= last)
        def _():
            for cp in weight_copies(blk_ref[jnp.minimum(next_first, last)], 1 - slot):
                cp.start()

        wgu_bf[...] = wgu_f32[slot].astype(BF16)
        wd_bf[...] = wd_f32[slot].astype(BF16)
        slot_ref[0] = 1 - slot

    @pl.when(i < nused_ref[0])
    def _():
        x = _load_row_tiles(x_ref, ROW_BLOCK).astype(BF16)
        gu = jnp.dot(x, wgu_bf[...], preferred_element_type=F32) + bgu_ref[0]
        gate = jnp.minimum(gu[:, 0:D_EXPERT], SWIGLU_LIMIT)
        up = jnp.clip(gu[:, D_EXPERT:2 * D_EXPERT], -SWIGLU_LIMIT, SWIGLU_LIMIT)
        act = (up + 1.0) * (gate * jax.nn.sigmoid(SWIGLU_ALPHA * gate))
        out = jnp.dot(act.astype(BF16), wd_bf[...], preferred_element_type=F32) + bd_ref[0]
        _store_row_tiles(o_ref, out, ROW_BLOCK)

    @pl.when(i >= nused_ref[0])
    def _():
        o_ref[...] = jnp.zeros_like(o_ref)


def _ffn(blk_exp, nused, run_end, xs, wgu, bgu, wd, bd):
    blk_rows = ROW_BLOCK * ROW_TILE
    n_blk = xs.shape[0] // blk_rows
    row_map = lambda i, blk, nu, re: (jnp.minimum(i, nu[0] - 1), 0)
    exp_map = lambda i, blk, nu, re: (blk[jnp.minimum(i, nu[0] - 1)], 0, 0)
    return pl.pallas_call(
        _ffn_body,
        grid_spec=pltpu.PrefetchScalarGridSpec(
            num_scalar_prefetch=3,
            grid=(n_blk,),
            in_specs=[
                pl.BlockSpec((blk_rows, LANES), row_map),
                pl.BlockSpec(memory_space=pl.ANY),
                pl.BlockSpec((1, 1, 2 * D_EXPERT), exp_map),
                pl.BlockSpec(memory_space=pl.ANY),
                pl.BlockSpec((1, 1, D_MODEL), exp_map),
            ],
            out_specs=pl.BlockSpec((blk_rows, LANES), lambda i, blk, nu, re: (i, 0)),
            scratch_shapes=[pltpu.VMEM((2, D_MODEL, 2 * D_EXPERT), F32), pltpu.VMEM((2, D_EXPERT, D_MODEL), F32),
                            pltpu.VMEM((D_MODEL, 2 * D_EXPERT), BF16), pltpu.VMEM((D_EXPERT, D_MODEL), BF16),
                            pltpu.SemaphoreType.DMA((2,)), pltpu.SMEM((1,), I32)],
        ),
        out_shape=jax.ShapeDtypeStruct(xs.shape, F32),
        compiler_params=pltpu.CompilerParams(dimension_semantics=("arbitrary",), vmem_limit_bytes=VMEM_LIMIT_BYTES),
        name="ffn",
    )(blk_exp, nused, run_end, xs, wgu, bgu, wd, bd)


def _combine_body(dnext_ref, dcur_ref, ys_ref, x1_ref, gate_ref, gf_ref, y_ref, buf, sems, *, n_tiles):
    i = pl.program_id(0)
    cur = lax.rem(i, 2)
    n_groups = TOK_TILE // COMBINE_GROUP

    def issue_group(dref, slot, g):
        for u in range(COMBINE_GROUP):
            r = g * COMBINE_GROUP + u
            for k in range(TOP_K):
                _tile_copy(ys_ref, dref[0, 0, k * TOK_TILE + r], buf.at[slot, k], r, sems.at[slot]).start(priority=k % 2)

    def finish_group(g):
        base = pl.multiple_of(g * COMBINE_GROUP, COMBINE_GROUP)
        gates = gate_ref[pl.ds(base, COMBINE_GROUP), :]
        moe = None
        for k in range(TOP_K):
            rows = jnp.concatenate(
                [buf[cur, k, pl.ds(base * ROW_TILE + j, COMBINE_GROUP, stride=ROW_TILE), :] for j in range(ROW_TILE)],
                axis=1)
            term = rows * gates[:, k:k + 1]
            moe = term if moe is None else moe + term
        x2 = x1_ref[pl.ds(base, COMBINE_GROUP), :] + moe
        ms = jnp.mean(x2 * x2, axis=-1, keepdims=True)
        y_ref[pl.ds(base, COMBINE_GROUP), :] = x2 * lax.rsqrt(ms + RMS_EPS) * gf_ref[...]

    @pl.when(i == 0)
    def _():
        def first(g, c):
            issue_group(dcur_ref, 0, g)
            return c

        lax.fori_loop(0, n_groups, first, 0)

    for k in range(TOP_K):
        pltpu.make_async_copy(ys_ref.at[pl.ds(0, TOK_TILE * ROW_TILE), :], buf.at[cur, k], sems.at[cur]).wait()

    @pl.when(i + 1 < n_tiles)
    def _():
        def both(g, c):
            issue_group(dnext_ref, 1 - cur, g)
            finish_group(g)
            return c

        lax.fori_loop(0, n_groups, both, 0)

    @pl.when(i + 1 >= n_tiles)
    def _():
        def last(g, c):
            finish_group(g)
            return c

        lax.fori_loop(0, n_groups, last, 0)


def _combine(dest_tiles, ys, x1, gates_t, gf, tile0):
    n = x1.shape[0]
    assert n % TOK_TILE == 0
    n_tiles = n // TOK_TILE
    dest_spec = lambda ahead: pl.BlockSpec(
        (1, 1, TOP_K * TOK_TILE), lambda i: (tile0 + jnp.minimum(i + ahead, n_tiles - 1), 0, 0), memory_space=pltpu.SMEM)
    return pl.pallas_call(
        functools.partial(_combine_body, n_tiles=n_tiles),
        grid=(n_tiles,),
        in_specs=[
            dest_spec(1), dest_spec(0),
            pl.BlockSpec(memory_space=pl.ANY),
            pl.BlockSpec((TOK_TILE, D_MODEL), lambda i: (i, 0)),
            pl.BlockSpec((TOK_TILE, TOP_K), lambda i: (tile0 + i, 0)),
            pl.BlockSpec((1, D_MODEL), lambda i: (0, 0)),
        ],
        out_specs=pl.BlockSpec((TOK_TILE, D_MODEL), lambda i: (i, 0)),
        out_shape=jax.ShapeDtypeStruct((n, D_MODEL), F32),
        scratch_shapes=[pltpu.VMEM((2, TOP_K, TOK_TILE * ROW_TILE, LANES), F32), pltpu.SemaphoreType.DMA((2,))],
        compiler_params=pltpu.CompilerParams(dimension_semantics=("arbitrary",), vmem_limit_bytes=VMEM_LIMIT_BYTES),
        name="combine",
    )(dest_tiles, dest_tiles, ys, x1, gates_t, gf)


def _pack_gate_weights(w_rg, w_ig):
    def bd(w):
        w = w.reshape(D_RNN // MXU_DIM, GATE_PACK, LRU_BLOCK, LRU_BLOCK)
        eye = jnp.eye(GATE_PACK, dtype=w.dtype)
        return jnp.einsum('jpik,pq->jpiqk', w, eye).reshape(D_RNN // MXU_DIM, MXU_DIM, MXU_DIM)
    return jnp.concatenate([bd(w_rg), bd(w_ig)], axis=-1).astype(BF16)


def kernel(x_prompt, x_sample, state_conv, state_h, norm1_g, w_in, conv_w, conv_b, w_rg, b_rg, w_ig, b_ig, lru_lambda, ln_v_g, ln_v_b, w_s, b_s, w_proj_a, w_proj_b, w_out, norm2_g, w_router, b_router, w_gate_up, b_gate_up, w_down, b_down, final_norm_g):
    assert norm1_g.shape[0] == 1, "single layer"
    B, S, _ = x_prompt.shape
    Bs, Ss, _ = x_sample.shape
    row = lambda p: p.reshape(1, -1).astype(F32)
    wts = (
        row(norm1_g[0]), w_in[0].astype(BF16), conv_w[0], row(conv_b[0]), _pack_gate_weights(w_rg[0], w_ig[0]),
        row(b_rg[0]), row(b_ig[0]), row(lru_lambda[0]), row(ln_v_g[0]), row(ln_v_b[0]), w_s[0], b_s[0].T,
        w_proj_a[0].astype(BF16), w_proj_b[0].astype(BF16), w_out[0].astype(BF16), row(norm2_g[0]),
        w_router[0].T.astype(BF16), b_router[0].reshape(N_EXPERTS, 1),
    )
    x1_p, xn2_p, lgt_p, convl_p, hl_p = _mixer(
        x_prompt, jnp.zeros((B, SUBLANES, D_RNN), F32), jnp.zeros((B, 1, D_RNN), F32), wts,
        nseq=1, seg=min(MIX_ROWS, S), start_zero=True, want_v=False)
    conv0_s = jnp.pad(state_conv[0], ((0, 0), (SUBLANES - (CONV_WIDTH - 1), 0), (0, 0)))
    x1_s, xn2_s, lgt_s, convl_s, hl_s, v_s = _mixer(
        x_sample, conv0_s, state_h[0][:, None, :], wts,
        nseq=Bs, seg=Ss, start_zero=False, want_v=True)

    n_p, n_s = B * S, Bs * Ss
    n_tok = n_p + n_s
    n_blk = -(-(n_tok * TOP_K + N_EXPERTS * (ROW_BLOCK - 1)) // ROW_BLOCK)
    _, gate, dest, blk_exp, cnt, pstart, nused = _route((lgt_p, lgt_s), n_blk)
    assert n_tok % TOK_TILE == 0
    n_tiles = n_tok // TOK_TILE
    dest_tiles = dest.reshape(TOP_K, n_tiles, TOK_TILE).transpose(1, 0, 2).reshape(n_tiles, 1, TOP_K * TOK_TILE)
    gates_t = gate.T
    cnt, pstart, nused, blk_exp = cnt.reshape(-1), pstart.reshape(-1), nused.reshape(-1), blk_exp.reshape(-1)

    xs = _dispatch(cnt, pstart, nused, dest_tiles, xn2_p, xn2_s, n_blk)
    run_end = (pstart + cnt + (ROW_BLOCK - 1)) // ROW_BLOCK
    ys = _ffn(blk_exp, nused, run_end, xs, w_gate_up[0], b_gate_up[0][:, None, :], w_down[0], b_down[0][:, None, :])
    gf = row(final_norm_g)
    y_p = _combine(dest_tiles, ys, x1_p.reshape(n_p, D_MODEL), gates_t, gf, 0).reshape(B, S, D_MODEL)
    y_s = _combine(dest_tiles, ys, x1_s.reshape(n_s, D_MODEL), gates_t, gf, n_p // TOK_TILE).reshape(Bs, Ss, D_MODEL)

    keep = slice(SUBLANES - (CONV_WIDTH - 1), SUBLANES)
    return (y_p, y_s, convl_p[None, :, keep, :], hl_p.reshape(1, B, D_RNN), convl_s[None, :, keep, :],
            hl_s.reshape(1, Bs, D_RNN), v_s[None])
```

```python
import functools

import jax
import jax.numpy as jnp
from jax import lax
from jax.experimental import pallas as pl
from jax.experimental.pallas import tpu as pltpu

F32 = jnp.float32
BF16 = jnp.bfloat16
I32 = jnp.int32

D_MODEL = 1024
D_RNN = D_MODEL
N_LRU_BLOCKS = 16
LRU_BLOCK = D_RNN // N_LRU_BLOCKS
CONV_WIDTH = 4
LRU_C = 8.0
MLP_CHUNK = 128
N_MLP_GROUPS = 4
MLP_GROUP_DIM = D_MODEL // 8
D_MLP = N_MLP_GROUPS * MLP_GROUP_DIM
N_IN = 2 * D_RNN + 2 * D_MLP + 2 * D_MODEL
N_EXPERTS = 32
TOP_K = 4
D_EXPERT = D_MODEL
SWIGLU_LIMIT = 7.0
SWIGLU_ALPHA = 1.702
RMS_EPS = 1e-6
LN_EPS = 1e-5

SUBLANES = 8
LANES = 128
MXU_DIM = 256
ROW_TILE = D_MODEL // LANES
VMEM_LIMIT_BYTES = 56 * 1024 * 1024

MIX_ROWS = 256
PROJ_CUTS = (1024, 3584, 4608)
ROUTE_CHUNK = 256
ROW_BLOCK = 512
TOK_TILE = 256
ISSUE_UNROLL = 8
COMBINE_GROUP = 256
GATE_PACK = MXU_DIM // LRU_BLOCK


def _const_spec(shape):
    nd = len(shape)
    return pl.BlockSpec(shape, lambda *_: (0,) * nd, pipeline_mode=pl.Buffered(1))


def _store_row_tiles(ref, val, n):
    for j in range(ROW_TILE):
        ref[pl.ds(j, n, stride=ROW_TILE), :] = val[:, j * LANES:(j + 1) * LANES]


def _load_row_tiles(ref, n):
    return jnp.concatenate([ref[pl.ds(j, n, stride=ROW_TILE), :] for j in range(ROW_TILE)], axis=1)


def _row_tile(ref, row):
    start = row * ROW_TILE
    if not isinstance(start, int):
        start = pl.multiple_of(start, ROW_TILE)
    return ref.at[pl.ds(start, ROW_TILE), :]


def _tile_copy(src, src_row, dst, dst_row, sem):
    return pltpu.make_async_copy(_row_tile(src, src_row), _row_tile(dst, dst_row), sem)


def _norm1(x, g1_ref):
    ms = jnp.mean(x * x, axis=-1, keepdims=True)
    return (x * lax.rsqrt(ms + RMS_EPS) * g1_ref[...]).astype(BF16)


def _project(x, g1_ref, win_ref):
    return jnp.dot(_norm1(x, g1_ref), win_ref[...], preferred_element_type=F32)


def _restart_state(conv0_ref, h0_ref, state):
    xa_buf, h_tiles, _, _ = state
    nseq = xa_buf.shape[0]
    xa_buf[:, 0:SUBLANES, :] = conv0_ref[...]
    _store_row_tiles(h_tiles, h0_ref[...].reshape(nseq, D_RNN), nseq)


def _mix_tile(x, z, s, wrefs, orefs, state, *, nseq, seg, mc, start_zero, want_v, fill_mxu=lambda slot: None):
    (convw_ref, convb_ref, wg_ref, brg_ref, big_ref, lam_ref, lng_ref, lnb_ref, ws_ref, bst_ref, wpa_ref, wpb_ref,
     wout_ref, g2_ref, wrt_ref, brt_ref) = wrefs
    if want_v:
        x1_ref, xn2_ref, lgt_ref, convl_ref, hl_ref, v_ref = orefs
    else:
        x1_ref, xn2_ref, lgt_ref, convl_ref, hl_ref = orefs
        v_ref = None
    xa_buf, h_tiles, scan_a, scan_b = state
    rows = nseq * seg
    xa = z[:, 0:D_RNN]
    ga = z[:, D_RNN:2 * D_RNN]
    zu = z[:, 2 * D_RNN:2 * D_RNN + D_MLP]
    zv = z[:, 2 * D_RNN + D_MLP:2 * D_RNN + 2 * D_MLP]
    gate_a = z[:, 2 * D_RNN + 2 * D_MLP:2 * D_RNN + 2 * D_MLP + D_MODEL]
    gate_b = z[:, 2 * D_RNN + 2 * D_MLP + D_MODEL:N_IN]

    cw = convw_ref[...]
    xc_parts = []
    for q in range(nseq):
        xa_buf[q, SUBLANES:SUBLANES + seg, :] = xa[q * seg:(q + 1) * seg]
        acc = convb_ref[...] + xa[q * seg:(q + 1) * seg] * cw[CONV_WIDTH - 1:CONV_WIDTH]
        for k in range(CONV_WIDTH - 1):
            off = SUBLANES - (CONV_WIDTH - 1) + k
            acc = acc + xa_buf[q, off:off + seg, :] * cw[k:k + 1]
        xc_parts.append(acc)
        xa_buf[q, 0:SUBLANES, :] = xa_buf[q, seg:seg + SUBLANES, :]
    xc = xc_parts[0] if nseq == 1 else jnp.concatenate(xc_parts, axis=0)
    convl_ref[...] = xa_buf[:, 0:SUBLANES, :]

    u = jax.nn.gelu(zu)
    gv = jax.nn.gelu(zv)
    mu = jnp.mean(gv, axis=-1, keepdims=True)
    vc = gv - mu
    var = jnp.mean(vc * vc, axis=-1, keepdims=True)
    v = vc * lax.rsqrt(var + LN_EPS) * lng_ref[...] + lnb_ref[...]
    if want_v:
        v_ref[...] = v.reshape(nseq, seg, D_MLP)
    vb = v.astype(BF16)

    xcb = xc.astype(BF16)
    r_parts, i_parts = [], []
    for j in range(D_RNN // MXU_DIM):
        gj = jnp.dot(xcb[:, j * MXU_DIM:(j + 1) * MXU_DIM], wg_ref[j], preferred_element_type=F32)
        r_parts.append(gj[:, 0:MXU_DIM])
        i_parts.append(gj[:, MXU_DIM:2 * MXU_DIM])
    fill_mxu(0)

    tri = lax.broadcasted_iota(I32, (mc, mc), 0) >= lax.broadcasted_iota(I32, (mc, mc), 1)
    ws_tri = [jnp.where(tri, ws_ref[g][0:mc, 0:mc], 0.0).astype(BF16) for g in range(N_MLP_GROUPS)]
    bst = bst_ref[...]
    mix_rows = []
    for c in range(rows // mc):
        cols = []
        for g in range(N_MLP_GROUPS):
            vg = vb[c * mc:(c + 1) * mc, g * MLP_GROUP_DIM:(g + 1) * MLP_GROUP_DIM]
            mg = jnp.dot(ws_tri[g], vg, preferred_element_type=F32) + bst[0:mc, g:g + 1]
            cols.append(mg)
        mix_rows.append(jnp.concatenate(cols, axis=1))
    vmix = mix_rows[0] if len(mix_rows) == 1 else jnp.concatenate(mix_rows, axis=0)

    r = jax.nn.sigmoid(jnp.concatenate(r_parts, axis=1) + brg_ref[...])
    ig = jax.nn.sigmoid(jnp.concatenate(i_parts, axis=1) + big_ref[...])
    lam = lam_ref[...]
    softplus_neg = jnp.maximum(-lam, 0.0) + jnp.log1p(jnp.exp(-jnp.abs(lam)))
    log_a = (-LRU_C) * r * softplus_neg
    a = jnp.exp(log_a)
    mult = jnp.sqrt(jnp.maximum(1.0 - a * a, 0.0))
    if start_zero:
        row = lax.broadcasted_iota(I32, (rows, 1), 0)
        mult = jnp.where(jnp.logical_and(row == 0, s == 0), 1.0, mult)
    bterm = mult * ig * xc

    _store_row_tiles(scan_a, a, rows)
    _store_row_tiles(scan_b, bterm, rows)
    for q in range(nseq):
        hq = h_tiles[q * ROW_TILE:(q + 1) * ROW_TILE, :]
        for t in range(seg):
            r = (q * seg + t) * ROW_TILE
            hq = scan_a[r:r + ROW_TILE, :] * hq + scan_b[r:r + ROW_TILE, :]
            scan_b[r:r + ROW_TILE, :] = hq
        h_tiles[q * ROW_TILE:(q + 1) * ROW_TILE, :] = hq
    h = _load_row_tiles(scan_b, rows)
    hl_ref[...] = _load_row_tiles(h_tiles, nseq).reshape(nseq, 1, D_RNN)

    ya = jnp.dot((h * jax.nn.gelu(ga)).astype(BF16), wpa_ref[...], preferred_element_type=F32)
    yb = jnp.dot((u * vmix).astype(BF16), wpb_ref[...], preferred_element_type=F32)
    fill_mxu(1)

    merged = jax.nn.sigmoid(gate_a) * ya + jax.nn.sigmoid(gate_b) * yb
    x1 = x + jnp.dot(merged.astype(BF16), wout_ref[...], preferred_element_type=F32)
    fill_mxu(2)
    x1_ref[...] = x1.reshape(nseq, seg, D_MODEL)

    ms2 = jnp.mean(x1 * x1, axis=-1, keepdims=True)
    xn2 = x1 * lax.rsqrt(ms2 + RMS_EPS) * g2_ref[...]
    _store_row_tiles(xn2_ref, xn2, rows)
    lgt = lax.dot_general(wrt_ref[...], xn2.astype(BF16), (((1,), (1,)), ((), ())), preferred_element_type=F32)
    lgt_ref[...] = lgt + brt_ref[...]


N_MIX_WEIGHTS = 18


def _mixer_whole_body(x_ref, conv0_ref, h0_ref, g1_ref, win_ref, *rest, n_out, **static):
    wrefs = rest[:N_MIX_WEIGHTS - 2]
    orefs = rest[N_MIX_WEIGHTS - 2:N_MIX_WEIGHTS - 2 + n_out]
    state = rest[N_MIX_WEIGHTS - 2 + n_out:]
    x = x_ref[...].reshape(static["nseq"] * static["seg"], D_MODEL)
    _restart_state(conv0_ref, h0_ref, state)
    _mix_tile(x, _project(x, g1_ref, win_ref), 0, wrefs, orefs, state, **static)


def _mixer_pipe_body(xnext_ref, x_ref, conv0_ref, h0_ref, g1_ref, win_ref, *rest, n_out, n_s, **static):
    wrefs = rest[:N_MIX_WEIGHTS - 2]
    orefs = rest[N_MIX_WEIGHTS - 2:N_MIX_WEIGHTS - 2 + n_out]
    *state, z_even, z_odd = rest[N_MIX_WEIGHTS - 2 + n_out:]
    t = pl.program_id(0)
    rows = static["seg"]
    s_prev = lax.rem(jnp.maximum(t - 1, 0), n_s)

    @pl.when(t == 0)
    def _():
        z_odd[...] = jnp.zeros_like(z_odd)

    @pl.when(s_prev == 0)
    def _():
        _restart_state(conv0_ref, h0_ref, state)

    def step(z_new, z_done):
        xn = _norm1(xnext_ref[...].reshape(rows, D_MODEL), g1_ref)

        def project(lo, hi):
            z_new[:, lo:hi] = jnp.dot(xn, win_ref[:, lo:hi], preferred_element_type=F32)

        cuts = (0,) + PROJ_CUTS + (N_IN,)
        project(cuts[0], cuts[1])
        _mix_tile(x_ref[...].reshape(rows, D_MODEL), z_done, s_prev, wrefs, orefs, state,
                  fill_mxu=lambda slot: project(cuts[slot + 1], cuts[slot + 2]), **static)

    @pl.when(lax.rem(t, 2) == 0)
    def _():
        step(z_even, z_odd)

    @pl.when(lax.rem(t, 2) == 1)
    def _():
        step(z_odd, z_even)


def _mixer(x, conv0, h0, wts, *, nseq, seg, start_zero, want_v):
    B, S, _ = x.shape
    assert len(wts) == N_MIX_WEIGHTS
    assert B % nseq == 0 and S % seg == 0 and seg % SUBLANES == 0
    assert nseq == 1 or (seg == S and nseq == B)
    mc = min(seg, MLP_CHUNK)
    assert seg % mc == 0
    rows = nseq * seg
    n_s = S // seg
    n_tiles = (B // nseq) * n_s
    pipelined = nseq == 1
    done = (lambda t: jnp.maximum(t - 1, 0)) if pipelined else (lambda t: t)
    seq_spec = lambda w, tile: pl.BlockSpec((nseq, seg, w), lambda t: (tile(t) // n_s, tile(t) % n_s, 0))
    state_spec = lambda r: pl.BlockSpec((nseq, r, D_RNN), lambda t: (done(t) // n_s, 0, 0))
    in_specs = [seq_spec(D_MODEL, done), state_spec(SUBLANES), state_spec(1)] + [_const_spec(w.shape) for w in wts]
    out_shape = [
        jax.ShapeDtypeStruct((B, S, D_MODEL), F32),
        jax.ShapeDtypeStruct((B * S * ROW_TILE, LANES), F32),
        jax.ShapeDtypeStruct((N_EXPERTS, B * S), F32),
        jax.ShapeDtypeStruct((B, SUBLANES, D_RNN), F32),
        jax.ShapeDtypeStruct((B, 1, D_RNN), F32),
    ]
    out_specs = [
        seq_spec(D_MODEL, done), pl.BlockSpec((rows * ROW_TILE, LANES), lambda t: (done(t), 0)),
        pl.BlockSpec((N_EXPERTS, rows), lambda t: (0, done(t))),
        state_spec(SUBLANES), state_spec(1),
    ]
    if want_v:
        out_shape.append(jax.ShapeDtypeStruct((B, S, D_MLP), F32))
        out_specs.append(seq_spec(D_MLP, done))
    static = dict(n_out=len(out_shape), nseq=nseq, seg=seg, mc=mc, start_zero=start_zero, want_v=want_v)
    scratch = [pltpu.VMEM((nseq, seg + SUBLANES, D_RNN), F32), pltpu.VMEM((nseq * ROW_TILE, LANES), F32),
               pltpu.VMEM((rows * ROW_TILE, LANES), F32), pltpu.VMEM((rows * ROW_TILE, LANES), F32)]
    if pipelined:
        body = functools.partial(_mixer_pipe_body, n_s=n_s, **static)
        grid = (n_tiles + 1,)
        in_specs = [seq_spec(D_MODEL, lambda t: jnp.minimum(t, n_tiles - 1))] + in_specs
        scratch += [pltpu.VMEM((rows, N_IN), F32), pltpu.VMEM((rows, N_IN), F32)]
        operands = (x, x, conv0, h0, *wts)
    else:
        body = functools.partial(_mixer_whole_body, **static)
        grid = (n_tiles,)
        operands = (x, conv0, h0, *wts)
    return pl.pallas_call(
        body,
        grid=grid,
        in_specs=in_specs,
        out_specs=out_specs,
        out_shape=out_shape,
        scratch_shapes=scratch,
        compiler_params=pltpu.CompilerParams(dimension_semantics=("arbitrary",), vmem_limit_bytes=VMEM_LIMIT_BYTES),
        name="mixer",
    )(*operands)


def _route_body(*refs, n_toks, n_blk_pad):
    lgt_refs = refs[:len(n_toks)]
    e_ref, gate_ref, dest_ref, blk_ref, cnt_ref, pstart_ref, nused_ref = refs[len(n_toks):]
    ch = ROUTE_CHUNK
    n_chunks = sum(n_toks) // ch
    eiota = lax.broadcasted_iota(I32, (N_EXPERTS, ch), 0)
    upper = (lax.broadcasted_iota(I32, (ch, ch), 0) < lax.broadcasted_iota(I32, (ch, ch), 1)).astype(BF16)

    def rank_chunk(lgt_ref, first_chunk, c, counts):
        off = pl.multiple_of((first_chunk + c) * ch, ch)
        work = lgt_ref[:, pl.ds(pl.multiple_of(c * ch, ch), ch)]
        sel = jnp.zeros((N_EXPERTS, ch), F32)
        es, vs = [], []
        for _ in range(TOP_K):
            m = jnp.max(work, axis=0, keepdims=True)
            idx = jnp.min(jnp.where(work == m, eiota, N_EXPERTS), axis=0, keepdims=True)
            hit = eiota == idx
            es.append(idx)
            vs.append(m)
            work = jnp.where(hit, -jnp.inf, work)
            sel = jnp.where(hit, 1.0, sel)
        exps = [jnp.exp(v - vs[0]) for v in vs]
        inv = 1.0 / (exps[0] + exps[1] + exps[2] + exps[3])
        before = jnp.dot(sel.astype(BF16), upper, preferred_element_type=F32) + counts
        for k in range(TOP_K):
            rank = jnp.sum(jnp.where(eiota == es[k], before, 0.0), axis=0, keepdims=True)
            e_ref[k:k + 1, pl.ds(off, ch)] = es[k]
            gate_ref[k:k + 1, pl.ds(off, ch)] = exps[k] * inv
            dest_ref[k:k + 1, pl.ds(off, ch)] = rank.astype(I32)
        return counts + jnp.sum(sel, axis=1, keepdims=True)

    counts = jnp.zeros((N_EXPERTS, 1), F32)
    first_chunk = 0
    for lgt_ref, n in zip(lgt_refs, n_toks):
        counts = lax.fori_loop(0, n // ch, functools.partial(rank_chunk, lgt_ref, first_chunk), counts)
        first_chunk += n // ch
    counts = counts.astype(I32)
    shift = ROW_BLOCK.bit_length() - 1
    padded = lax.shift_left(lax.shift_right_logical(counts + (ROW_BLOCK - 1), shift), shift)
    col = lax.broadcasted_iota(I32, (N_EXPERTS, 1), 0)
    pad_end = jnp.zeros((N_EXPERTS, 1), I32)
    for e in range(N_EXPERTS):
        tot = jnp.sum(jnp.where(col <= e, padded, 0), axis=0, keepdims=True)
        pad_end = jnp.where(col == e, tot, pad_end)
    pad_start = pad_end - padded
    cnt_ref[...] = counts
    pstart_ref[...] = pad_start
    nused_ref[...] = lax.shift_right_logical(jnp.max(pad_end, axis=0, keepdims=True), shift)
    blk_row = lax.broadcasted_iota(I32, (N_EXPERTS, n_blk_pad), 1) * ROW_BLOCK
    blk_ref[...] = jnp.minimum(jnp.sum((pad_end <= blk_row).astype(I32), axis=0, keepdims=True), N_EXPERTS - 1)

    def place_chunk(c, carry):
        off = pl.multiple_of(c * ch, ch)
        for k in range(TOP_K):
            ek = e_ref[k:k + 1, pl.ds(off, ch)]
            base = jnp.sum(jnp.where(eiota == ek, pad_start, 0), axis=0, keepdims=True)
            dest_ref[k:k + 1, pl.ds(off, ch)] = dest_ref[k:k + 1, pl.ds(off, ch)] + base
        return carry

    lax.fori_loop(0, n_chunks, place_chunk, 0)


def _route(lgts, n_blk):
    n_toks = tuple(lgt.shape[1] for lgt in lgts)
    assert all(n % ROUTE_CHUNK == 0 for n in n_toks)
    n_tok = sum(n_toks)
    n_blk_pad = -(-n_blk // LANES) * LANES
    full = lambda shape: pl.BlockSpec(shape, lambda i: (0,) * len(shape))
    return pl.pallas_call(
        functools.partial(_route_body, n_toks=n_toks, n_blk_pad=n_blk_pad),
        grid=(1,),
        in_specs=[full(lgt.shape) for lgt in lgts],
        out_specs=[full((TOP_K, n_tok)), full((TOP_K, n_tok)), full((TOP_K, n_tok)), full((1, n_blk_pad)),
                   full((N_EXPERTS, 1)), full((N_EXPERTS, 1)), full((1, 1))],
        out_shape=[jax.ShapeDtypeStruct((TOP_K, n_tok), I32),
                   jax.ShapeDtypeStruct((TOP_K, n_tok), F32),
                   jax.ShapeDtypeStruct((TOP_K, n_tok), I32),
                   jax.ShapeDtypeStruct((1, n_blk_pad), I32),
                   jax.ShapeDtypeStruct((N_EXPERTS, 1), I32),
                   jax.ShapeDtypeStruct((N_EXPERTS, 1), I32),
                   jax.ShapeDtypeStruct((1, 1), I32)],
        compiler_params=pltpu.CompilerParams(dimension_semantics=("arbitrary",), vmem_limit_bytes=VMEM_LIMIT_BYTES),
        name="route",
    )(*lgts)


def _dispatch_body(cnt_ref, pstart_ref, nused_ref, dest_ref, xp_ref, xs_ref, out_ref, zbuf, sem, *,
                   n_prompt_tiles, n_blk):
    i = pl.program_id(0)

    @pl.when(i == 0)
    def _():
        zbuf[...] = jnp.zeros_like(zbuf)
        sizes = [1 << b for b in reversed(range(ROW_BLOCK.bit_length() - 1))]

        def pad_copies(e):
            cnt = cnt_ref[e]
            first = pstart_ref[e] + cnt
            n_pad = (-cnt) & (ROW_BLOCK - 1)
            for size in sizes:
                start = first + (n_pad & ~(2 * size - 1))
                dst = out_ref.at[pl.ds(pl.multiple_of(start * ROW_TILE, ROW_TILE), size * ROW_TILE), :]
                yield (n_pad & size) != 0, pltpu.make_async_copy(zbuf.at[0:size * ROW_TILE, :], dst, sem)

        for e in range(N_EXPERTS):
            for needed, cp in pad_copies(e):
                pl.when(needed)(cp.start)
        for e in range(N_EXPERTS):
            for needed, cp in pad_copies(e):
                pl.when(needed)(cp.wait)

        def blk_copy(b):
            blk_rows = ROW_BLOCK * ROW_TILE
            return pltpu.make_async_copy(
                zbuf, out_ref.at[pl.ds(pl.multiple_of(b * blk_rows, blk_rows), blk_rows), :], sem)

        def fill_blk(b, c):
            blk_copy(b).start()
            return c

        def drain_blk(b, c):
            blk_copy(b).wait()
            return c

        lax.fori_loop(nused_ref[0], n_blk, fill_blk, 0)
        lax.fori_loop(nused_ref[0], n_blk, drain_blk, 0)

    def scatter_rows(src):
        def issue(g, c):
            for u in range(ISSUE_UNROLL):
                r = g * ISSUE_UNROLL + u
                for k in range(TOP_K):
                    _tile_copy(src, r, out_ref, dest_ref[0, 0, k * TOK_TILE + r], sem).start(priority=k % 2)
            return c

        lax.fori_loop(0, TOK_TILE // ISSUE_UNROLL, issue, 0)
        for k in range(TOP_K):
            pltpu.make_async_copy(src, out_ref.at[pl.ds(0, TOK_TILE * ROW_TILE), :], sem).wait()

    @pl.when(i < n_prompt_tiles)
    def _():
        scatter_rows(xp_ref)

    @pl.when(i >= n_prompt_tiles)
    def _():
        scatter_rows(xs_ref)


def _dispatch(cnt, pstart, nused, dest_tiles, xn2_p, xn2_s, n_blk):
    tile_rows = TOK_TILE * ROW_TILE
    n_p = xn2_p.shape[0] // tile_rows
    n_s = xn2_s.shape[0] // tile_rows
    assert xn2_p.shape[0] % tile_rows == 0 and xn2_s.shape[0] % tile_rows == 0
    return pl.pallas_call(
        functools.partial(_dispatch_body, n_prompt_tiles=n_p, n_blk=n_blk),
        grid_spec=pltpu.PrefetchScalarGridSpec(
            num_scalar_prefetch=3,
            grid=(n_p + n_s,),
            in_specs=[
                pl.BlockSpec((1, 1, TOP_K * TOK_TILE), lambda i, *_: (i, 0, 0), memory_space=pltpu.SMEM),
                pl.BlockSpec((tile_rows, LANES), lambda i, *_: (jnp.minimum(i, n_p - 1), 0)),
                pl.BlockSpec((tile_rows, LANES), lambda i, *_: (jnp.maximum(i - n_p, 0), 0)),
            ],
            out_specs=pl.BlockSpec(memory_space=pl.ANY),
            scratch_shapes=[pltpu.VMEM((ROW_BLOCK * ROW_TILE, LANES), F32), pltpu.SemaphoreType.DMA],
        ),
        out_shape=jax.ShapeDtypeStruct((n_blk * ROW_BLOCK * ROW_TILE, LANES), F32),
        compiler_params=pltpu.CompilerParams(dimension_semantics=("arbitrary",), vmem_limit_bytes=VMEM_LIMIT_BYTES),
        name="dispatch",
    )(cnt, pstart, nused, dest_tiles, xn2_p, xn2_s)


def _ffn_body(blk_ref, nused_ref, runend_ref, x_ref, wgu_hbm, bgu_ref, wd_hbm, bd_ref, o_ref, wgu_f32, wd_f32,
              wgu_bf, wd_bf, sems, slot_ref):
    i = pl.program_id(0)
    last = nused_ref[0] - 1
    expert = blk_ref[jnp.minimum(i, last)]
    prev_expert = blk_ref[jnp.minimum(jnp.maximum(i - 1, 0), last)]

    def weight_copies(e, slot):
        return (pltpu.make_async_copy(wgu_hbm.at[e], wgu_f32.at[slot], sems.at[slot]),
                pltpu.make_async_copy(wd_hbm.at[e], wd_f32.at[slot], sems.at[slot]))

    @pl.when(i == 0)
    def _():
        slot_ref[0] = 0
        for cp in weight_copies(expert, 0):
            cp.start()

    @pl.when(jnp.logical_or(i == 0, expert != prev_expert))
    def _():
        slot = slot_ref[0]
        for cp in weight_copies(expert, slot):
            cp.wait()
        next_first = runend_ref[expert]

        @pl.when(next_first <= last)
        def _():
            for cp in weight_copies(blk_ref[jnp.minimum(next_first, last)], 1 - slot):
                cp.start()

        wgu_bf[...] = wgu_f32[slot].astype(BF16)
        wd_bf[...] = wd_f32[slot].astype(BF16)
        slot_ref[0] = 1 - slot

    @pl.when(i < nused_ref[0])
    def _():
        x = _load_row_tiles(x_ref, ROW_BLOCK).astype(BF16)
        gu = jnp.dot(x, wgu_bf[...], preferred_element_type=F32) + bgu_ref[0]
        gate = jnp.minimum(gu[:, 0:D_EXPERT], SWIGLU_LIMIT)
        up = jnp.clip(gu[:, D_EXPERT:2 * D_EXPERT], -SWIGLU_LIMIT, SWIGLU_LIMIT)
        act = (up + 1.0) * (gate * jax.nn.sigmoid(SWIGLU_ALPHA * gate))
        out = jnp.dot(act.astype(BF16), wd_bf[...], preferred_element_type=F32) + bd_ref[0]
        _store_row_tiles(o_ref, out, ROW_BLOCK)

    @pl.when(i >= nused_ref[0])
    def _():
        o_ref[...] = jnp.zeros_like(o_ref)


def _ffn(blk_exp, nused, run_end, xs, wgu, bgu, wd, bd):
    blk_rows = ROW_BLOCK * ROW_TILE
    n_blk = xs.shape[0] // blk_rows
    row_map = lambda i, blk, nu, re: (jnp.minimum(i, nu[0] - 1), 0)
    exp_map = lambda i, blk, nu, re: (blk[jnp.minimum(i, nu[0] - 1)], 0, 0)
    return pl.pallas_call(
        _ffn_body,
        grid_spec=pltpu.PrefetchScalarGridSpec(
            num_scalar_prefetch=3,
            grid=(n_blk,),
            in_specs=[
                pl.BlockSpec((blk_rows, LANES), row_map),
                pl.BlockSpec(memory_space=pl.ANY),
                pl.BlockSpec((1, 1, 2 * D_EXPERT), exp_map),
                pl.BlockSpec(memory_space=pl.ANY),
                pl.BlockSpec((1, 1, D_MODEL), exp_map),
            ],
            out_specs=pl.BlockSpec((blk_rows, LANES), lambda i, blk, nu, re: (i, 0)),
            scratch_shapes=[pltpu.VMEM((2, D_MODEL, 2 * D_EXPERT), F32), pltpu.VMEM((2, D_EXPERT, D_MODEL), F32),
                            pltpu.VMEM((D_MODEL, 2 * D_EXPERT), BF16), pltpu.VMEM((D_EXPERT, D_MODEL), BF16),
                            pltpu.SemaphoreType.DMA((2,)), pltpu.SMEM((1,), I32)],
        ),
        out_shape=jax.ShapeDtypeStruct(xs.shape, F32),
        compiler_params=pltpu.CompilerParams(dimension_semantics=("arbitrary",), vmem_limit_bytes=VMEM_LIMIT_BYTES),
        name="ffn",
    )(blk_exp, nused, run_end, xs, wgu, bgu, wd, bd)


def _combine_body(dnext_ref, dcur_ref, ys_ref, x1_ref, gate_ref, gf_ref, y_ref, buf, sems, *, n_tiles):
    i = pl.program_id(0)
    cur = lax.rem(i, 2)
    n_groups = TOK_TILE // COMBINE_GROUP

    def issue_group(dref, slot, g):
        for u in range(COMBINE_GROUP):
            r = g * COMBINE_GROUP + u
            for k in range(TOP_K):
                _tile_copy(ys_ref, dref[0, 0, k * TOK_TILE + r], buf.at[slot, k], r, sems.at[slot]).start(priority=k % 2)

    def finish_group(g):
        base = pl.multiple_of(g * COMBINE_GROUP, COMBINE_GROUP)
        gates = gate_ref[pl.ds(base, COMBINE_GROUP), :]
        moe = None
        for k in range(TOP_K):
            rows = jnp.concatenate(
                [buf[cur, k, pl.ds(base * ROW_TILE + j, COMBINE_GROUP, stride=ROW_TILE), :] for j in range(ROW_TILE)],
                axis=1)
            term = rows * gates[:, k:k + 1]
            moe = term if moe is None else moe + term
        x2 = x1_ref[pl.ds(base, COMBINE_GROUP), :] + moe
        ms = jnp.mean(x2 * x2, axis=-1, keepdims=True)
        y_ref[pl.ds(base, COMBINE_GROUP), :] = x2 * lax.rsqrt(ms + RMS_EPS) * gf_ref[...]

    @pl.when(i == 0)
    def _():
        def first(g, c):
            issue_group(dcur_ref, 0, g)
            return c

        lax.fori_loop(0, n_groups, first, 0)

    for k in range(TOP_K):
        pltpu.make_async_copy(ys_ref.at[pl.ds(0, TOK_TILE * ROW_TILE), :], buf.at[cur, k], sems.at[cur]).wait()

    @pl.when(i + 1 < n_tiles)
    def _():
        def both(g, c):
            issue_group(dnext_ref, 1 - cur, g)
            finish_group(g)
            return c

        lax.fori_loop(0, n_groups, both, 0)

    @pl.when(i + 1 >= n_tiles)
    def _():
        def last(g, c):
            finish_group(g)
            return c

        lax.fori_loop(0, n_groups, last, 0)


def _combine(dest_tiles, ys, x1, gates_t, gf, tile0):
    n = x1.shape[0]
    assert n % TOK_TILE == 0
    n_tiles = n // TOK_TILE
    dest_spec = lambda ahead: pl.BlockSpec(
        (1, 1, TOP_K * TOK_TILE), lambda i: (tile0 + jnp.minimum(i + ahead, n_tiles - 1), 0, 0), memory_space=pltpu.SMEM)
    return pl.pallas_call(
        functools.partial(_combine_body, n_tiles=n_tiles),
        grid=(n_tiles,),
        in_specs=[
            dest_spec(1), dest_spec(0),
            pl.BlockSpec(memory_space=pl.ANY),
            pl.BlockSpec((TOK_TILE, D_MODEL), lambda i: (i, 0)),
            pl.BlockSpec((TOK_TILE, TOP_K), lambda i: (tile0 + i, 0)),
            pl.BlockSpec((1, D_MODEL), lambda i: (0, 0)),
        ],
        out_specs=pl.BlockSpec((TOK_TILE, D_MODEL), lambda i: (i, 0)),
        out_shape=jax.ShapeDtypeStruct((n, D_MODEL), F32),
        scratch_shapes=[pltpu.VMEM((2, TOP_K, TOK_TILE * ROW_TILE, LANES), F32), pltpu.SemaphoreType.DMA((2,))],
        compiler_params=pltpu.CompilerParams(dimension_semantics=("arbitrary",), vmem_limit_bytes=VMEM_LIMIT_BYTES),
        name="combine",
    )(dest_tiles, dest_tiles, ys, x1, gates_t, gf)


def _pack_gate_weights(w_rg, w_ig):
    def bd(w):
        w = w.reshape(D_RNN // MXU_DIM, GATE_PACK, LRU_BLOCK, LRU_BLOCK)
        eye = jnp.eye(GATE_PACK, dtype=w.dtype)
        return jnp.einsum('jpik,pq->jpiqk', w, eye).reshape(D_RNN // MXU_DIM, MXU_DIM, MXU_DIM)
    return jnp.concatenate([bd(w_rg), bd(w_ig)], axis=-1).astype(BF16)


def kernel(x_prompt, x_sample, state_conv, state_h, norm1_g, w_in, conv_w, conv_b, w_rg, b_rg, w_ig, b_ig, lru_lambda, ln_v_g, ln_v_b, w_s, b_s, w_proj_a, w_proj_b, w_out, norm2_g, w_router, b_router, w_gate_up, b_gate_up, w_down, b_down, final_norm_g):
    assert norm1_g.shape[0] == 1, "single layer"
    B, S, _ = x_prompt.shape
    Bs, Ss, _ = x_sample.shape
    row = lambda p: p.reshape(1, -1).astype(F32)
    wts = (
        row(norm1_g[0]), w_in[0].astype(BF16), conv_w[0], row(conv_b[0]), _pack_gate_weights(w_rg[0], w_ig[0]),
        row(b_rg[0]), row(b_ig[0]), row(lru_lambda[0]), row(ln_v_g[0]), row(ln_v_b[0]), w_s[0], b_s[0].T,
        w_proj_a[0].astype(BF16), w_proj_b[0].astype(BF16), w_out[0].astype(BF16), row(norm2_g[0]),
        w_router[0].T.astype(BF16), b_router[0].reshape(N_EXPERTS, 1),
    )
    x1_p, xn2_p, lgt_p, convl_p, hl_p = _mixer(
        x_prompt, jnp.zeros((B, SUBLANES, D_RNN), F32), jnp.zeros((B, 1, D_RNN), F32), wts,
        nseq=1, seg=min(MIX_ROWS, S), start_zero=True, want_v=False)
    conv0_s = jnp.pad(state_conv[0], ((0, 0), (SUBLANES - (CONV_WIDTH - 1), 0), (0, 0)))
    x1_s, xn2_s, lgt_s, convl_s, hl_s, v_s = _mixer(
        x_sample, conv0_s, state_h[0][:, None, :], wts,
        nseq=Bs, seg=Ss, start_zero=False, want_v=True)

    n_p, n_s = B * S, Bs * Ss
    n_tok = n_p + n_s
    n_blk = -(-(n_tok * TOP_K + N_EXPERTS * (ROW_BLOCK - 1)) // ROW_BLOCK)
    _, gate, dest, blk_exp, cnt, pstart, nused = _route((lgt_p, lgt_s), n_blk)
    assert n_tok % TOK_TILE == 0
    n_tiles = n_tok // TOK_TILE
    dest_tiles = dest.reshape(TOP_K, n_tiles, TOK_TILE).transpose(1, 0, 2).reshape(n_tiles, 1, TOP_K * TOK_TILE)
    gates_t = gate.T
    cnt, pstart, nused, blk_exp = cnt.reshape(-1), pstart.reshape(-1), nused.reshape(-1), blk_exp.reshape(-1)

    xs = _dispatch(cnt, pstart, nused, dest_tiles, xn2_p, xn2_s, n_blk)
    run_end = (pstart + cnt + (ROW_BLOCK - 1)) // ROW_BLOCK
    ys = _ffn(blk_exp, nused, run_end, xs, w_gate_up[0], b_gate_up[0][:, None, :], w_down[0], b_down[0][:, None, :])
    gf = row(final_norm_g)
    y_p = _combine(dest_tiles, ys, x1_p.reshape(n_p, D_MODEL), gates_t, gf, 0).reshape(B, S, D_MODEL)
    y_s = _combine(dest_tiles, ys, x1_s.reshape(n_s, D_MODEL), gates_t, gf, n_p // TOK_TILE).reshape(Bs, Ss, D_MODEL)

    keep = slice(SUBLANES - (CONV_WIDTH - 1), SUBLANES)
    return (y_p, y_s, convl_p[None, :, keep, :], hl_p.reshape(1, B, D_RNN), convl_s[None, :, keep, :],
            hl_s.reshape(1, Bs, D_RNN), v_s[None])
```
